```python
import jax, jax.numpy as jnp
from jax import lax
import numpy as np

D_MODEL = 1024
BATCH = 8
SEQ = 2048
DEPTH = 4

CHUNK = 64
Q_BLOCK = 128
EPS = 1e-6

CONV_WIDTH = 512
CONV_KERNEL = 31

FOX_HEADS = 8
FOX_HEAD_DIM = 64
FOX_WIDTH = FOX_HEADS * FOX_HEAD_DIM

LRU_WIDTH = 512
LRU_BLOCKS = 8
LRU_BLOCK_DIM = LRU_WIDTH // LRU_BLOCKS
LRU_CONV = 4
LRU_C = 8.0

N_BRANCH = 3
FFN_HIDDEN = -(-8 * D_MODEL // (3 * 256)) * 256

IN_SIZES = (2 * CONV_WIDTH, FOX_WIDTH, FOX_WIDTH, FOX_WIDTH, FOX_HEADS,
            LRU_WIDTH, LRU_WIDTH, N_BRANCH * D_MODEL)
IN_COLS = sum(IN_SIZES)

kernel_name = "hybrid_conv_fox_rglru_gated_trunk"


def rms_norm(x, g):
    xf = x.astype(jnp.float32)
    y = xf * lax.rsqrt(jnp.mean(xf * xf, axis=-1, keepdims=True) + EPS)
    return (y * g.astype(jnp.float32)).astype(x.dtype)


def layer_norm(x, g, b):
    xf = x.astype(jnp.float32)
    mu = jnp.mean(xf, axis=-1, keepdims=True)
    var = jnp.mean(jnp.square(xf - mu), axis=-1, keepdims=True)
    y = (xf - mu) * lax.rsqrt(var + EPS)
    return (y * g.astype(jnp.float32) + b.astype(jnp.float32)).astype(x.dtype)


def causal_depthwise_conv(u, w, b):
    k = w.shape[0]
    y = lax.conv_general_dilated(
        u, w[:, None, :].astype(u.dtype), window_strides=(1,), padding=[(k - 1, 0)],
        dimension_numbers=("NWC", "WIO", "NWC"), feature_group_count=u.shape[-1])
    return y + b


def conformer_conv_branch(u_glu, dw_w, dw_b, ln_g, ln_b, w_out):
    a, g = jnp.split(u_glu, 2, axis=-1)
    v = a * jax.nn.sigmoid(g)
    v = causal_depthwise_conv(v, dw_w, dw_b)
    v = jax.nn.silu(layer_norm(v, ln_g, ln_b))
    return v @ w_out


def forgetting_attention_branch(q, k, v, f_logit, b_f, qn_g, kn_g, w_out):
    bsz, seq, _ = q.shape
    shp = (bsz, seq, FOX_HEADS, FOX_HEAD_DIM)
    qh = rms_norm(q.reshape(shp), qn_g).transpose(0, 2, 1, 3)
    kh = rms_norm(k.reshape(shp), kn_g).transpose(0, 2, 1, 3)
    vh = v.reshape(shp).transpose(0, 2, 1, 3)
    log_f = jax.nn.log_sigmoid(f_logit.astype(jnp.float32) + b_f.astype(jnp.float32))
    cum = jnp.cumsum(log_f, axis=1).transpose(0, 2, 1)
    scale = FOX_HEAD_DIM ** -0.5
    outs = []
    for q0 in range(0, seq, Q_BLOCK):
        q1 = q0 + Q_BLOCK
        logits = jnp.einsum("bhqd,bhkd->bhqk", qh[:, :, q0:q1], kh[:, :, :q1]).astype(jnp.float32) * scale
        logits = logits + cum[:, :, q0:q1, None] - cum[:, :, None, :q1]
        causal = (q0 + jnp.arange(Q_BLOCK))[:, None] >= jnp.arange(q1)[None, :]
        logits = jnp.where(causal, logits, -jnp.inf)
        p = jax.nn.softmax(logits, axis=-1).astype(vh.dtype)
        outs.append(jnp.einsum("bhqk,bhkd->bhqd", p, vh[:, :, :q1]))
    o = jnp.concatenate(outs, axis=2).transpose(0, 2, 1, 3).reshape(bsz, seq, FOX_WIDTH)
    return o @ w_out


def rg_lru_branch(y_in, x_in, conv_w, conv_b, w_r, b_r, w_i, b_i, lam, w_out):
    bsz, seq, _ = x_in.shape
    xr = causal_depthwise_conv(x_in, conv_w, conv_b)
    xb = xr.reshape(bsz, seq, LRU_BLOCKS, LRU_BLOCK_DIM)
    r = jax.nn.sigmoid(jnp.einsum("bsnd,nde->bsne", xb, w_r).reshape(bsz, seq, LRU_WIDTH) + b_r)
    i = jax.nn.sigmoid(jnp.einsum("bsnd,nde->bsne", xb, w_i).reshape(bsz, seq, LRU_WIDTH) + b_i)
    log_a = -LRU_C * r.astype(jnp.float32) * jax.nn.softplus(-lam.astype(jnp.float32))
    a = jnp.exp(log_a)
    u = jnp.sqrt(-jnp.expm1(2.0 * log_a)) * (i * xr).astype(jnp.float32)

    def combine(lhs, rhs):
        a1, b1 = lhs
        a2, b2 = rhs
        return a1 * a2, a2 * b1 + b2

    _, h = lax.associative_scan(combine, (a, u), axis=1)
    return (jax.nn.gelu(y_in) * h.astype(x_in.dtype)) @ w_out


def setup_inputs(seed: int = 0) -> dict:
    key = jax.random.key(seed)
    keys = jax.random.split(key, 32)
    counter = [0]

    def next_key():
        k = keys[counter[0]]
        counter[0] += 1
        return k

    def nrm(shape, scale):
        return jax.random.normal(next_key(), shape, jnp.float32) * scale

    L = DEPTH
    d = D_MODEL
    x = nrm((BATCH, SEQ, d), 1.0)
    c = nrm((BATCH, d), 1.0)
    w_ada = nrm((L, d, 6 * d), 0.5 * d ** -0.5)
    b_ada = nrm((L, 6 * d), 0.02)
    g_norm_mix = 1.0 + nrm((L, d), 0.05)
    g_norm_ffn = 1.0 + nrm((L, d), 0.05)
    w_in = nrm((L, d, IN_COLS), d ** -0.5)
    conv_dw_w = nrm((L, CONV_KERNEL, CONV_WIDTH), CONV_KERNEL ** -0.5)
    conv_dw_b = nrm((L, CONV_WIDTH), 0.02)
    conv_ln_g = 1.0 + nrm((L, CONV_WIDTH), 0.05)
    conv_ln_b = nrm((L, CONV_WIDTH), 0.02)
    w_conv_out = nrm((L, CONV_WIDTH, d), CONV_WIDTH ** -0.5)
    fox_b_f = 2.0 + nrm((L, FOX_HEADS), 0.5)
    fox_q_norm_g = 1.0 + nrm((L, FOX_HEAD_DIM), 0.05)
    fox_k_norm_g = 1.0 + nrm((L, FOX_HEAD_DIM), 0.05)
    w_fox_out = nrm((L, FOX_WIDTH, d), FOX_WIDTH ** -0.5)
    lru_conv_w = nrm((L, LRU_CONV, LRU_WIDTH), LRU_CONV ** -0.5)
    lru_conv_b = nrm((L, LRU_WIDTH), 0.02)
    lru_w_r = nrm((L, LRU_BLOCKS, LRU_BLOCK_DIM, LRU_BLOCK_DIM), LRU_BLOCK_DIM ** -0.5)
    lru_b_r = nrm((L, LRU_WIDTH), 0.02)
    lru_w_i = nrm((L, LRU_BLOCKS, LRU_BLOCK_DIM, LRU_BLOCK_DIM), LRU_BLOCK_DIM ** -0.5)
    lru_b_i = nrm((L, LRU_WIDTH), 0.02)
    a_c = jax.random.uniform(next_key(), (L, LRU_WIDTH), jnp.float32, 0.9, 0.999)
    a0 = a_c ** (1.0 / LRU_C)
    lru_lambda = jnp.log(a0) - jnp.log1p(-a0)
    w_lru_out = nrm((L, LRU_WIDTH, d), LRU_WIDTH ** -0.5)
    w_o = nrm((L, d, d), d ** -0.5)
    w_ffn_in = nrm((L, d, 2 * FFN_HIDDEN), d ** -0.5)
    w_ffn_out = nrm((L, FFN_HIDDEN, d), FFN_HIDDEN ** -0.5)
    return {
        "x": x, "c": c, "w_ada": w_ada, "b_ada": b_ada,
        "g_norm_mix": g_norm_mix, "g_norm_ffn": g_norm_ffn, "w_in": w_in,
        "conv_dw_w": conv_dw_w, "conv_dw_b": conv_dw_b, "conv_ln_g": conv_ln_g,
        "conv_ln_b": conv_ln_b, "w_conv_out": w_conv_out,
        "fox_b_f": fox_b_f, "fox_q_norm_g": fox_q_norm_g, "fox_k_norm_g": fox_k_norm_g,
        "w_fox_out": w_fox_out,
        "lru_conv_w": lru_conv_w, "lru_conv_b": lru_conv_b, "lru_w_r": lru_w_r,
        "lru_b_r": lru_b_r, "lru_w_i": lru_w_i, "lru_b_i": lru_b_i,
        "lru_lambda": lru_lambda, "w_lru_out": w_lru_out,
        "w_o": w_o, "w_ffn_in": w_ffn_in, "w_ffn_out": w_ffn_out,
    }


def reference(x, c, w_ada, b_ada, g_norm_mix, g_norm_ffn, w_in,
              conv_dw_w, conv_dw_b, conv_ln_g, conv_ln_b, w_conv_out,
              fox_b_f, fox_q_norm_g, fox_k_norm_g, w_fox_out,
              lru_conv_w, lru_conv_b, lru_w_r, lru_b_r, lru_w_i, lru_b_i,
              lru_lambda, w_lru_out, w_o, w_ffn_in, w_ffn_out):
    bsz, seq, _ = x.shape
    assert seq % CHUNK == 0 and seq % Q_BLOCK == 0
    split_at = [int(s) for s in np.cumsum(IN_SIZES)[:-1]]
    c_act = jax.nn.silu(c)
    for l in range(DEPTH):
        mod = c_act @ w_ada[l] + b_ada[l]
        sh1, sc1, gt1, sh2, sc2, gt2 = [m[:, None, :] for m in jnp.split(mod, 6, axis=-1)]

        h = rms_norm(x, g_norm_mix[l]) * (1.0 + sc1) + sh1
        z = h @ w_in[l]
        u_glu, q, k, v, f_logit, y_in, x_in, g_logit = jnp.split(z, split_at, axis=-1)
        br_a = conformer_conv_branch(u_glu, conv_dw_w[l], conv_dw_b[l],
                                     conv_ln_g[l], conv_ln_b[l], w_conv_out[l])
        br_b = forgetting_attention_branch(q, k, v, f_logit, fox_b_f[l],
                                           fox_q_norm_g[l], fox_k_norm_g[l], w_fox_out[l])
        br_c = rg_lru_branch(y_in, x_in, lru_conv_w[l], lru_conv_b[l], lru_w_r[l], lru_b_r[l],
                             lru_w_i[l], lru_b_i[l], lru_lambda[l], w_lru_out[l])
        gates = jax.nn.sigmoid(g_logit).reshape(bsz, seq, N_BRANCH, D_MODEL)
        merged = gates[:, :, 0] * br_a + gates[:, :, 1] * br_b + gates[:, :, 2] * br_c
        x = x + gt1 * (merged @ w_o[l])

        h = rms_norm(x, g_norm_ffn[l]) * (1.0 + sc2) + sh2
        a_ffn, b_ffn = jnp.split(h @ w_ffn_in[l], 2, axis=-1)
        x = x + gt2 * ((jax.nn.silu(a_ffn) * b_ffn) @ w_ffn_out[l])
    return x
```

```python
import functools

import jax
import jax.numpy as jnp
import numpy as np
from jax import lax
from jax.experimental import pallas as pl
from jax.experimental.pallas import tpu as pltpu

F32 = jnp.float32
BF16 = jnp.bfloat16

D_MODEL = 1024
EPS = 1e-6
CONV_WIDTH = 512
CONV_KERNEL = 31
FOX_HEADS = 8
FOX_HEAD_DIM = 64
FOX_WIDTH = FOX_HEADS * FOX_HEAD_DIM
LRU_WIDTH = 512
LRU_BLOCKS = 8
LRU_CONV = 4
LRU_C = 8.0
N_BRANCH = 3
FFN_HIDDEN = 2816

LANES = 128
SUBLANES = 8
HEAD_SLOT = LANES
QK_WIDTH = FOX_HEADS * HEAD_SLOT
CUM_PARTS = 3
CONV_HALO = 32
LRU_HALO = SUBLANES
VMEM_LIMIT = 56 * 1024 * 1024

TS_PROJ = 256
TS_LOCAL = 256
TQ = 256
TK = 256
TS_MLP = 512
CONV_ROWS = 32
FFN_CHUNKS = ((0, 1024), (1024, 1024), (2048, 768))


def _dot(a, b):
  return jnp.dot(a, b, preferred_element_type=F32)


def _sigmoid(x):
  return 1.0 / (1.0 + jnp.exp(-x))


def _softplus(x):
  return jnp.maximum(x, 0.0) + jnp.log1p(jnp.exp(-jnp.abs(x)))


def _split_bf16(x, parts):
  out = []
  rem = x
  for _ in range(parts):
    p = rem.astype(BF16)
    out.append(p)
    rem = rem - p.astype(F32)
  return out


def _resident(shape):
  nd = len(shape)
  return pl.BlockSpec(shape, lambda *_: (0,) * nd, pipeline_mode=pl.Buffered(1))


def _params(*sem):
  return pltpu.CompilerParams(dimension_semantics=sem, vmem_limit_bytes=VMEM_LIMIT)


def _ada_kernel(c_ref, w_ref, b_ref, o_ref):
  c = c_ref[...]
  act = (c * _sigmoid(c)).astype(BF16)
  o_ref[...] = _dot(act, w_ref[...].astype(BF16)) + b_ref[...]


def _ada_call(c, w_ada, b_ada):
  depth, d, n = w_ada.shape
  bsz = c.shape[0]
  tn = d
  return pl.pallas_call(
      _ada_kernel,
      out_shape=jax.ShapeDtypeStruct((depth, bsz, n), F32),
      grid=(depth, n // tn),
      in_specs=[
          pl.BlockSpec((bsz, d), lambda l, j: (0, 0)),
          pl.BlockSpec((None, d, tn), lambda l, j: (l, 0, j)),
          pl.BlockSpec((None, 1, tn), lambda l, j: (l, 0, j)),
      ],
      out_specs=pl.BlockSpec((None, bsz, tn), lambda l, j: (l, 0, j)),
      compiler_params=_params("arbitrary", "arbitrary"),
      name="ada",
  )(c, w_ada, b_ada.reshape(depth, 1, n))


def _modulated_norm(x, g, sc, sh):
  ms = jnp.mean(x * x, axis=-1, keepdims=True)
  h = (x * lax.rsqrt(ms + EPS)) * g
  return h * (1.0 + sc) + sh


def _head_norm(z):
  parts = []
  for hh in range(FOX_HEADS):
    s = z[:, hh * HEAD_SLOT:(hh + 1) * HEAD_SLOT]
    ssq = jnp.sum(s * s, axis=-1, keepdims=True)
    parts.append(s * lax.rsqrt(ssq * (1.0 / FOX_HEAD_DIM) + EPS))
  return jnp.concatenate(parts, axis=-1)


def _inproj_kernel(x_ref, sc_ref, sh_ref, g_ref,
                   wglu_ref, wq_ref, wk_ref, wv_ref, wf_ref, wyx_ref, wg_ref,
                   bf_ref, gq_ref, gk_ref, tri_ref, plq_ref, plk_ref,
                   vg_ref, q_ref, k_ref, v_ref, y_ref, xin_ref, gate_ref,
                   carry_ref):
  ts = x_ref.shape[0]
  t = pl.program_id(1)
  hb = _modulated_norm(x_ref[...], g_ref[...], sc_ref[...], sh_ref[...]).astype(BF16)

  zg = _dot(hb, wglu_ref[...])
  vg_ref[...] = zg[:, :CONV_WIDTH] * _sigmoid(zg[:, CONV_WIDTH:])

  zf = _dot(hb, wf_ref[...]) + bf_ref[...]
  lf = jnp.minimum(zf, 0.0) - jnp.log1p(jnp.exp(-jnp.abs(zf)))

  @pl.when(t == 0)
  def _():
    carry_ref[...] = jnp.zeros_like(carry_ref)

  tri = tri_ref[...]
  cum = carry_ref[...]
  for piece in _split_bf16(lf, CUM_PARTS):
    cum = cum + _dot(tri, piece)
  carry_ref[...] = cum[ts - 1:ts, :]

  lane = lax.broadcasted_iota(jnp.int32, cum.shape, 1)
  pieces = _split_bf16(cum, CUM_PARTS)
  src = jnp.where(lane == FOX_HEADS * CUM_PARTS, 1.0, 0.0)
  for i in reversed(range(CUM_PARTS)):
    src = jnp.where(lane < FOX_HEADS * (i + 1), pieces[i].astype(F32), src)
  src = src.astype(BF16)

  qn = _head_norm(_dot(hb, wq_ref[...])) * gq_ref[...]
  q_ref[...] = (qn + _dot(src, plq_ref[...])).astype(BF16)
  kn = _head_norm(_dot(hb, wk_ref[...])) * gk_ref[...]
  k_ref[...] = (kn + _dot(src, plk_ref[...])).astype(BF16)

  v_ref[...] = _dot(hb, wv_ref[...]).astype(BF16)

  zyx = _dot(hb, wyx_ref[...])
  y_ref[...] = zyx[:, :LRU_WIDTH]
  xin_ref[...] = zyx[:, LRU_WIDTH:]

  gate_ref[...] = _sigmoid(_dot(hb, wg_ref[...]))


def _inproj_call(x, sc, sh, g, w, consts):
  bsz, seq, d = x.shape
  ts = TS_PROJ
  grid = (bsz, seq // ts)
  row = lambda width: pl.BlockSpec((None, ts, width), lambda b, t: (b, t, 0))
  per_batch = pl.BlockSpec((None, 1, d), lambda b, t: (b, 0, 0))
  weights = [w["glu"], w["q"], w["k"], w["v"], w["f"], w["yx"], w["gate"],
             w["bf"], w["gq"], w["gk"], consts["tri"], consts["plq"], consts["plk"]]
  out_widths = [(CONV_WIDTH, F32), (QK_WIDTH, BF16), (QK_WIDTH, BF16), (FOX_WIDTH, BF16),
                (LRU_WIDTH, F32), (LRU_WIDTH, F32), (N_BRANCH * d, F32)]
  return pl.pallas_call(
      _inproj_kernel,
      out_shape=[jax.ShapeDtypeStruct((bsz, seq, wd), dt) for wd, dt in out_widths],
      grid=grid,
      in_specs=[row(d), per_batch, per_batch, _resident((1, d))]
      + [_resident(a.shape) for a in weights],
      out_specs=[row(wd) for wd, _ in out_widths],
      scratch_shapes=[pltpu.VMEM((1, LANES), F32)],
      compiler_params=_params("arbitrary", "arbitrary"),
      name="in_proj",
  )(x, sc, sh, g, *weights)


def _gelu_tanh(x):
  c = np.float32(np.sqrt(2.0 / np.pi))
  return 0.5 * x * (1.0 + jnp.tanh(c * (x + 0.044715 * (x * x * x))))


def _linear_scan(a, u):
  ts = a.shape[0]
  row = lax.broadcasted_iota(jnp.int32, a.shape, 0)
  d = 1
  while d < ts:
    keep = row >= d
    u_prev = jnp.where(keep, pltpu.roll(u, d, 0), 0.0)
    a_prev = jnp.where(keep, pltpu.roll(a, d, 0), 1.0)
    u = u + a * u_prev
    a = a * a_prev
    d *= 2
  return a, u


def _local_kernel(vg_ref, y_ref, xin_ref, ga_ref, gc_ref,
                  cw_ref, cb_ref, lng_ref, lnb_ref, pa_ref,
                  lw_ref, lb_ref, wri_ref, bri_ref, lam_ref, pc_ref,
                  o_ref, cbuf, cvo, lbuf, hcar):
  ts = vg_ref.shape[0]
  t = pl.program_id(1)

  @pl.when(t == 0)
  def _():
    cbuf[0:CONV_HALO, :] = jnp.zeros((CONV_HALO, CONV_WIDTH), F32)
    lbuf[0:LRU_HALO, :] = jnp.zeros((LRU_HALO, LRU_WIDTH), F32)
    hcar[...] = jnp.zeros_like(hcar)

  @pl.when(t > 0)
  def _():
    cbuf[0:CONV_HALO, :] = cbuf[ts:ts + CONV_HALO, :]
    lbuf[0:LRU_HALO, :] = lbuf[ts:ts + LRU_HALO, :]

  cbuf[CONV_HALO:CONV_HALO + ts, :] = vg_ref[...]
  lbuf[LRU_HALO:LRU_HALO + ts, :] = xin_ref[...]

  first = CONV_HALO - (CONV_KERNEL - 1)
  for r0 in range(0, ts, CONV_ROWS):
    acc = jnp.broadcast_to(cb_ref[...], (CONV_ROWS, CONV_WIDTH))
    for j in range(CONV_KERNEL):
      lo = r0 + first + j
      acc = acc + cw_ref[j:j + 1, :] * cbuf[lo:lo + CONV_ROWS, :]
    cvo[r0:r0 + CONV_ROWS, :] = acc

  cv = cvo[...]
  mu = jnp.mean(cv, axis=-1, keepdims=True)
  cen = cv - mu
  var = jnp.mean(cen * cen, axis=-1, keepdims=True)
  ln = (cen * lax.rsqrt(var + EPS)) * lng_ref[...] + lnb_ref[...]
  br_a = _dot((ln * _sigmoid(ln)).astype(BF16), pa_ref[...])

  first = LRU_HALO - (LRU_CONV - 1)
  xr = jnp.broadcast_to(lb_ref[...], (ts, LRU_WIDTH))
  for j in range(LRU_CONV):
    xr = xr + lw_ref[j:j + 1, :] * lbuf[first + j:first + j + ts, :]
  ri = _dot(xr.astype(BF16), wri_ref[...]) + bri_ref[...]
  r = _sigmoid(ri[:, :LRU_WIDTH])
  i = _sigmoid(ri[:, LRU_WIDTH:])
  log_a = (-LRU_C * r) * _softplus(-lam_ref[...])
  a = jnp.exp(log_a)
  th = jnp.tanh(log_a)
  u = jnp.sqrt(-2.0 * th / (1.0 - th)) * (i * xr)
  a_run, h = _linear_scan(a, u)
  h = h + a_run * hcar[...]
  hcar[...] = h[ts - 1:ts, :]
  br_c = _dot((_gelu_tanh(y_ref[...]) * h).astype(BF16), pc_ref[...])

  o_ref[...] = ga_ref[...] * br_a + gc_ref[...] * br_c


def _local_call(vg, y, xin, gates, w):
  bsz, seq, _ = vg.shape
  d = D_MODEL
  ts = TS_LOCAL
  row = lambda width, col=0: pl.BlockSpec((None, ts, width), lambda b, t: (b, t, col))
  weights = [w["cw"], w["cb"], w["lng"], w["lnb"], w["pa"],
             w["lw"], w["lb"], w["wri"], w["bri"], w["lam"], w["pc"]]
  return pl.pallas_call(
      _local_kernel,
      out_shape=jax.ShapeDtypeStruct((bsz, seq, d), F32),
      grid=(bsz, seq // ts),
      in_specs=[row(CONV_WIDTH), row(LRU_WIDTH), row(LRU_WIDTH), row(d, 0), row(d, 2)]
      + [_resident(a.shape) for a in weights],
      out_specs=row(d),
      scratch_shapes=[
          pltpu.VMEM((CONV_HALO + ts, CONV_WIDTH), F32),
          pltpu.VMEM((ts, CONV_WIDTH), F32),
          pltpu.VMEM((LRU_HALO + ts, LRU_WIDTH), F32),
          pltpu.VMEM((1, LRU_WIDTH), F32),
      ],
      compiler_params=_params("arbitrary", "arbitrary"),
      name="local",
  )(vg, y, xin, gates, gates, *weights)


NEG_BIG = -1e30


def _attn_kernel(q_ref, k_ref, v_ref, o_ref, m_ref, l_ref, acc_ref):
  tq = q_ref.shape[0]
  tk = k_ref.shape[0]
  i = pl.program_id(2)
  j = pl.program_id(3)

  @pl.when(j == 0)
  def _():
    m_ref[...] = jnp.full(m_ref.shape, NEG_BIG, F32)
    l_ref[...] = jnp.zeros_like(l_ref)
    acc_ref[...] = jnp.zeros_like(acc_ref)

  def step(masked):
    v = v_ref[...]
    for hh in range(2):
      q = q_ref[:, hh * HEAD_SLOT:(hh + 1) * HEAD_SLOT]
      k = k_ref[:, hh * HEAD_SLOT:(hh + 1) * HEAD_SLOT]
      s = lax.dot_general(q, k, (((1,), (1,)), ((), ())), preferred_element_type=F32)
      if masked:
        row = lax.broadcasted_iota(jnp.int32, s.shape, 0)
        col = lax.broadcasted_iota(jnp.int32, s.shape, 1)
        s = jnp.where(row >= col, s, NEG_BIG)
      m_old = m_ref[hh]
      m_new = jnp.maximum(m_old, jnp.max(s, axis=-1, keepdims=True))
      alpha = jnp.exp(m_old - m_new)
      p = jnp.exp(s - m_new)
      l_ref[hh] = alpha * l_ref[hh] + jnp.sum(p, axis=-1, keepdims=True)
      acc_ref[hh] = alpha * acc_ref[hh] + _dot(p.astype(BF16), v)
      m_ref[hh] = m_new

  @pl.when(j < i)
  def _():
    step(False)

  @pl.when(j == i)
  def _():
    step(True)
    lane = lax.broadcasted_iota(jnp.int32, (tq, HEAD_SLOT), 1)
    o0 = acc_ref[0] / l_ref[0]
    o1 = acc_ref[1] / l_ref[1]
    o_ref[...] = jnp.where(lane < FOX_HEAD_DIM, o0, o1).astype(BF16)


def _attn_call(q, k, v):
  bsz, seq, _ = q.shape
  assert TQ == TK
  grid = (bsz, FOX_HEADS // 2, seq // TQ, seq // TK)
  return pl.pallas_call(
      _attn_kernel,
      out_shape=jax.ShapeDtypeStruct((bsz, seq, FOX_WIDTH), BF16),
      grid=grid,
      in_specs=[
          pl.BlockSpec((None, TQ, 2 * HEAD_SLOT), lambda b, h, i, j: (b, i, h)),
          pl.BlockSpec((None, TK, 2 * HEAD_SLOT), lambda b, h, i, j: (b, jnp.minimum(j, i), h)),
          pl.BlockSpec((None, TK, 2 * FOX_HEAD_DIM), lambda b, h, i, j: (b, jnp.minimum(j, i), h)),
      ],
      out_specs=pl.BlockSpec((None, TQ, 2 * FOX_HEAD_DIM), lambda b, h, i, j: (b, i, h)),
      scratch_shapes=[
          pltpu.VMEM((2, TQ, 1), F32),
          pltpu.VMEM((2, TQ, 1), F32),
          pltpu.VMEM((2, TQ, HEAD_SLOT), F32),
      ],
      compiler_params=_params("arbitrary", "arbitrary", "arbitrary", "arbitrary"),
      name="attn",
  )(q, k, v)


def _mlp_kernel(x_ref, o_ref, gb_ref, mac_ref, gt1_ref, sc_ref, sh_ref, gt2_ref, g_ref,
                pb_ref, wo_ref, w1_ref, w2_ref, out_ref):
  br_b = _dot(o_ref[...], pb_ref[...])
  merged = mac_ref[...] + gb_ref[...] * br_b
  x1 = x_ref[...] + gt1_ref[...] * _dot(merged.astype(BF16), wo_ref[...])
  hb = _modulated_norm(x1, g_ref[...], sc_ref[...], sh_ref[...]).astype(BF16)
  ffn = None
  for c0, cn in FFN_CHUNKS:
    a = _dot(hb, w1_ref[:, c0:c0 + cn])
    b = _dot(hb, w1_ref[:, FFN_HIDDEN + c0:FFN_HIDDEN + c0 + cn])
    part = _dot(((a * _sigmoid(a)) * b).astype(BF16), w2_ref[c0:c0 + cn, :])
    ffn = part if ffn is None else ffn + part
  out_ref[...] = x1 + gt2_ref[...] * ffn


def _mlp_call(x, o, gates, mac, gt1, sc2, sh2, gt2, g, w):
  bsz, seq, d = x.shape
  ts = TS_MLP
  row = lambda width, col=0: pl.BlockSpec((None, ts, width), lambda b, t: (b, t, col))
  per_batch = pl.BlockSpec((None, 1, d), lambda b, t: (b, 0, 0))
  weights = [w["pb"], w["wo"], w["w1"], w["w2"]]
  return pl.pallas_call(
      _mlp_kernel,
      out_shape=jax.ShapeDtypeStruct((bsz, seq, d), F32),
      grid=(bsz, seq // ts),
      in_specs=[row(d), row(FOX_WIDTH), row(d, 1), row(d),
                per_batch, per_batch, per_batch, per_batch, _resident((1, d))]
      + [_resident(a.shape) for a in weights],
      out_specs=row(d),
      compiler_params=_params("arbitrary", "arbitrary"),
      name="mlp",
  )(x, o, gates, mac, gt1, sc2, sh2, gt2, g, *weights)


def _pad_heads(w):
  d = w.shape[0]
  w = w.reshape(d, FOX_HEADS, FOX_HEAD_DIM)
  w = jnp.pad(w, ((0, 0), (0, 0), (0, HEAD_SLOT - FOX_HEAD_DIM)))
  return w.reshape(d, QK_WIDTH)


def _block_diag(w):
  n, a, b = w.shape
  eye = jnp.eye(n, dtype=w.dtype)
  return (eye[:, None, :, None] * w[:, :, None, :]).reshape(n * a, n * b)


def _placement_consts(ts):
  plq = np.zeros((LANES, QK_WIDTH), np.float32)
  plk = np.zeros((LANES, QK_WIDTH), np.float32)
  one_lane = FOX_HEADS * CUM_PARTS
  for hh in range(FOX_HEADS):
    base = hh * HEAD_SLOT + FOX_HEAD_DIM
    for i in range(CUM_PARTS):
      plq[FOX_HEADS * i + hh, base + i] = 1.0
      plq[one_lane, base + CUM_PARTS + i] = 1.0
      plk[one_lane, base + i] = 1.0
      plk[FOX_HEADS * i + hh, base + CUM_PARTS + i] = -1.0
  tri = np.tril(np.ones((ts, ts), np.float32))
  return {"plq": jnp.asarray(plq, BF16), "plk": jnp.asarray(plk, BF16),
          "tri": jnp.asarray(tri, BF16)}


def _layer_weights(l, p):
  d = D_MODEL
  w_in = p["w_in"][l]
  o_glu = 0
  o_q = o_glu + 2 * CONV_WIDTH
  o_k = o_q + FOX_WIDTH
  o_v = o_k + FOX_WIDTH
  o_f = o_v + FOX_WIDTH
  o_y = o_f + FOX_HEADS
  o_g = o_y + 2 * LRU_WIDTH
  wf = w_in[:, o_f:o_y]
  wf = jnp.pad(jnp.tile(wf, (1, CUM_PARTS)), ((0, 0), (0, LANES - CUM_PARTS * FOX_HEADS)))
  bf = jnp.pad(jnp.tile(p["fox_b_f"][l], CUM_PARTS), (0, LANES - CUM_PARTS * FOX_HEADS))
  scale = FOX_HEAD_DIM ** -0.5
  gq = _pad_heads(jnp.tile(p["fox_q_norm_g"][l] * scale, FOX_HEADS)[None, :])
  gk = _pad_heads(jnp.tile(p["fox_k_norm_g"][l], FOX_HEADS)[None, :])
  row = lambda v: v[None, :]
  return {
      "glu": w_in[:, o_glu:o_q].astype(BF16),
      "q": _pad_heads(w_in[:, o_q:o_k]).astype(BF16),
      "k": _pad_heads(w_in[:, o_k:o_v]).astype(BF16),
      "v": w_in[:, o_v:o_f].astype(BF16),
      "f": wf.astype(BF16),
      "yx": w_in[:, o_y:o_g].astype(BF16),
      "gate": w_in[:, o_g:].astype(BF16),
      "bf": row(bf), "gq": gq, "gk": gk,
      "cw": p["conv_dw_w"][l], "cb": row(p["conv_dw_b"][l]),
      "lng": row(p["conv_ln_g"][l]), "lnb": row(p["conv_ln_b"][l]),
      "pa": p["w_conv_out"][l].astype(BF16),
      "lw": p["lru_conv_w"][l], "lb": row(p["lru_conv_b"][l]),
      "wri": jnp.concatenate([_block_diag(p["lru_w_r"][l]), _block_diag(p["lru_w_i"][l])],
                             axis=1).astype(BF16),
      "bri": row(jnp.concatenate([p["lru_b_r"][l], p["lru_b_i"][l]])),
      "lam": row(p["lru_lambda"][l]),
      "pc": p["w_lru_out"][l].astype(BF16),
      "pb": p["w_fox_out"][l].astype(BF16),
      "wo": p["w_o"][l].astype(BF16),
      "w1": p["w_ffn_in"][l].astype(BF16),
      "w2": p["w_ffn_out"][l].astype(BF16),
  }


def kernel(x, c, w_ada, b_ada, g_norm_mix, g_norm_ffn, w_in, conv_dw_w, conv_dw_b, conv_ln_g, conv_ln_b, w_conv_out, fox_b_f, fox_q_norm_g, fox_k_norm_g, w_fox_out, lru_conv_w, lru_conv_b, lru_w_r, lru_b_r, lru_w_i, lru_b_i, lru_lambda, w_lru_out, w_o, w_ffn_in, w_ffn_out):
  p = dict(w_in=w_in, conv_dw_w=conv_dw_w, conv_dw_b=conv_dw_b, conv_ln_g=conv_ln_g,
           conv_ln_b=conv_ln_b, w_conv_out=w_conv_out, fox_b_f=fox_b_f,
           fox_q_norm_g=fox_q_norm_g, fox_k_norm_g=fox_k_norm_g, w_fox_out=w_fox_out,
           lru_conv_w=lru_conv_w, lru_conv_b=lru_conv_b, lru_w_r=lru_w_r, lru_b_r=lru_b_r,
           lru_w_i=lru_w_i, lru_b_i=lru_b_i, lru_lambda=lru_lambda, w_lru_out=w_lru_out,
           w_o=w_o, w_ffn_in=w_ffn_in, w_ffn_out=w_ffn_out)
  bsz, seq, d = x.shape
  depth = w_ada.shape[0]
  assert d == D_MODEL and seq % max(TS_PROJ, TS_LOCAL, TQ, TS_MLP) == 0
  consts = _placement_consts(TS_PROJ)
  mod = _ada_call(c, w_ada, b_ada)
  for l in range(depth):
    sh1, sc1, gt1, sh2, sc2, gt2 = [m[:, None, :] for m in jnp.split(mod[l], 6, axis=-1)]
    w = _layer_weights(l, p)
    vg, q, k, v, y, xin, gates = _inproj_call(x, sc1, sh1, g_norm_mix[l][None, :], w, consts)
    mac = _local_call(vg, y, xin, gates, w)
    o = _attn_call(q, k, v)
    x = _mlp_call(x, o, gates, mac, gt1, sc2, sh2, gt2, g_norm_ffn[l][None, :], w)
  return x
```

```python
import functools

import jax
import jax.numpy as jnp
import numpy as np
from jax import lax
from jax.experimental import pallas as pl
from jax.experimental.pallas import tpu as pltpu

F32 = jnp.float32
BF16 = jnp.bfloat16

D_MODEL = 1024
EPS = 1e-6
CONV_WIDTH = 512
CONV_KERNEL = 31
FOX_HEADS = 8
FOX_HEAD_DIM = 64
FOX_WIDTH = FOX_HEADS * FOX_HEAD_DIM
LRU_WIDTH = 512
LRU_BLOCKS = 8
LRU_CONV = 4
LRU_C = 8.0
N_BRANCH = 3
FFN_HIDDEN = 2816

LANES = 128
SUBLANES = 8
HEAD_SLOT = LANES
QK_WIDTH = FOX_HEADS * HEAD_SLOT
CUM_PARTS = 3
CONV_HALO = 32
LRU_HALO = SUBLANES
VMEM_LIMIT = 56 * 1024 * 1024
LOG2E = float(np.log2(np.e))

TS_PROJ = 256
TS_LOCAL = 256
TQ = 1024
TK = 256
TS_MLP = 512
CONV_ROWS = 32
FFN_CHUNKS = ((0, 1024), (1024, 1024), (2048, 768))


def _dot(a, b):
  return jnp.dot(a, b, preferred_element_type=F32)


def _sigmoid(x):
  return 1.0 / (1.0 + jnp.exp(-x))


def _softplus(x):
  return jnp.maximum(x, 0.0) + jnp.log1p(jnp.exp(-jnp.abs(x)))


def _split_bf16(x, parts):
  out = []
  rem = x
  for _ in range(parts):
    p = rem.astype(BF16)
    out.append(p)
    rem = rem - p.astype(F32)
  return out


def _resident(shape):
  nd = len(shape)
  return pl.BlockSpec(shape, lambda *_: (0,) * nd, pipeline_mode=pl.Buffered(1))


def _params(*sem):
  return pltpu.CompilerParams(dimension_semantics=sem, vmem_limit_bytes=VMEM_LIMIT)


def _ada_kernel(c_ref, w_ref, b_ref, o_ref):
  c = c_ref[...]
  act = (c * _sigmoid(c)).astype(BF16)
  o_ref[...] = _dot(act, w_ref[...].astype(BF16)) + b_ref[...]


def _ada_call(c, w_ada, b_ada):
  depth, d, n = w_ada.shape
  bsz = c.shape[0]
  tn = d
  return pl.pallas_call(
      _ada_kernel,
      out_shape=jax.ShapeDtypeStruct((depth, bsz, n), F32),
      grid=(depth, n // tn),
      in_specs=[
          pl.BlockSpec((bsz, d), lambda l, j: (0, 0)),
          pl.BlockSpec((None, d, tn), lambda l, j: (l, 0, j)),
          pl.BlockSpec((None, 1, tn), lambda l, j: (l, 0, j)),
      ],
      out_specs=pl.BlockSpec((None, bsz, tn), lambda l, j: (l, 0, j)),
      compiler_params=_params("arbitrary", "arbitrary"),
      name="ada",
  )(c, w_ada, b_ada.reshape(depth, 1, n))


def _modulated_norm(x, g, sc, sh):
  ms = jnp.mean(x * x, axis=-1, keepdims=True)
  h = (x * lax.rsqrt(ms + EPS)) * g
  return h * (1.0 + sc) + sh


def _head_norm(z):
  parts = []
  for hh in range(FOX_HEADS):
    s = z[:, hh * HEAD_SLOT:(hh + 1) * HEAD_SLOT]
    ssq = jnp.sum(s * s, axis=-1, keepdims=True)
    parts.append(s * lax.rsqrt(ssq * (1.0 / FOX_HEAD_DIM) + EPS))
  return jnp.concatenate(parts, axis=-1)


def _inproj_kernel(x_ref, sc_ref, sh_ref, g_ref,
                   wglu_ref, wq_ref, wk_ref, wv_ref, wf_ref, wyx_ref, wg_ref,
                   bf_ref, gq_ref, gk_ref, tri_ref, plq_ref, plk_ref,
                   vg_ref, q_ref, k_ref, v_ref, y_ref, xin_ref, gate_ref,
                   carry_ref):
  ts = x_ref.shape[0]
  t = pl.program_id(1)
  hb = _modulated_norm(x_ref[...], g_ref[...], sc_ref[...], sh_ref[...]).astype(BF16)

  zg = _dot(hb, wglu_ref[...])
  vg_ref[...] = zg[:, :CONV_WIDTH] * _sigmoid(zg[:, CONV_WIDTH:])

  zf = _dot(hb, wf_ref[...]) + bf_ref[...]
  lf = jnp.minimum(zf, 0.0) - jnp.log1p(jnp.exp(-jnp.abs(zf)))

  @pl.when(t == 0)
  def _():
    carry_ref[...] = jnp.zeros_like(carry_ref)

  tri = tri_ref[...]
  cum = carry_ref[...]
  for piece in _split_bf16(lf, CUM_PARTS):
    cum = cum + _dot(tri, piece)
  carry_ref[...] = cum[ts - 1:ts, :]

  lane = lax.broadcasted_iota(jnp.int32, cum.shape, 1)
  pieces = _split_bf16(cum * LOG2E, CUM_PARTS)
  src = jnp.where(lane == FOX_HEADS * CUM_PARTS, 1.0, 0.0)
  for i in reversed(range(CUM_PARTS)):
    src = jnp.where(lane < FOX_HEADS * (i + 1), pieces[i].astype(F32), src)
  src = src.astype(BF16)

  qn = _head_norm(_dot(hb, wq_ref[...])) * gq_ref[...]
  q_ref[...] = (qn + _dot(src, plq_ref[...])).astype(BF16)
  kn = _head_norm(_dot(hb, wk_ref[...])) * gk_ref[...]
  k_ref[...] = (kn + _dot(src, plk_ref[...])).astype(BF16)

  v_ref[...] = _dot(hb, wv_ref[...]).astype(BF16)

  zyx = _dot(hb, wyx_ref[...])
  y_ref[...] = zyx[:, :LRU_WIDTH]
  xin_ref[...] = zyx[:, LRU_WIDTH:]

  gate_ref[...] = _sigmoid(_dot(hb, wg_ref[...]))


def _inproj_call(x, sc, sh, g, w, consts):
  bsz, seq, d = x.shape
  ts = TS_PROJ
  grid = (bsz, seq // ts)
  row = lambda width: pl.BlockSpec((None, ts, width), lambda b, t: (b, t, 0))
  per_batch = pl.BlockSpec((None, 1, d), lambda b, t: (b, 0, 0))
  weights = [w["glu"], w["q"], w["k"], w["v"], w["f"], w["yx"], w["gate"],
             w["bf"], w["gq"], w["gk"], consts["tri"], consts["plq"], consts["plk"]]
  out_widths = [(CONV_WIDTH, F32), (QK_WIDTH, BF16), (QK_WIDTH, BF16), (FOX_WIDTH, BF16),
                (LRU_WIDTH, F32), (LRU_WIDTH, F32), (N_BRANCH * d, F32)]
  return pl.pallas_call(
      _inproj_kernel,
      out_shape=[jax.ShapeDtypeStruct((bsz, seq, wd), dt) for wd, dt in out_widths],
      grid=grid,
      in_specs=[row(d), per_batch, per_batch, _resident((1, d))]
      + [_resident(a.shape) for a in weights],
      out_specs=[row(wd) for wd, _ in out_widths],
      scratch_shapes=[pltpu.VMEM((1, LANES), F32)],
      compiler_params=_params("arbitrary", "arbitrary"),
      name="in_proj",
  )(x, sc, sh, g, *weights)


def _gelu_tanh(x):
  c = np.float32(np.sqrt(2.0 / np.pi))
  return 0.5 * x * (1.0 + jnp.tanh(c * (x + 0.044715 * (x * x * x))))


def _linear_scan(a, u):
  ts = a.shape[0]
  row = lax.broadcasted_iota(jnp.int32, a.shape, 0)
  d = 1
  while d < ts:
    keep = row >= d
    u_prev = jnp.where(keep, pltpu.roll(u, d, 0), 0.0)
    a_prev = jnp.where(keep, pltpu.roll(a, d, 0), 1.0)
    u = u + a * u_prev
    a = a * a_prev
    d *= 2
  return a, u


def _local_kernel(vg_ref, y_ref, xin_ref, ga_ref, gc_ref,
                  cw_ref, cb_ref, lng_ref, lnb_ref, pa_ref,
                  lw_ref, lb_ref, wri_ref, bri_ref, lam_ref, pc_ref,
                  o_ref, cbuf, csh, cvo, lbuf, hcar):
  ts = vg_ref.shape[0]
  t = pl.program_id(1)

  @pl.when(t == 0)
  def _():
    cbuf[0:CONV_HALO, :] = jnp.zeros((CONV_HALO, CONV_WIDTH), F32)
    lbuf[0:LRU_HALO, :] = jnp.zeros((LRU_HALO, LRU_WIDTH), F32)
    hcar[...] = jnp.zeros_like(hcar)

  @pl.when(t > 0)
  def _():
    cbuf[0:CONV_HALO, :] = cbuf[ts:ts + CONV_HALO, :]
    lbuf[0:LRU_HALO, :] = lbuf[ts:ts + LRU_HALO, :]

  cbuf[CONV_HALO:CONV_HALO + ts, :] = vg_ref[...]
  lbuf[LRU_HALO:LRU_HALO + ts, :] = xin_ref[...]

  first = CONV_HALO - (CONV_KERNEL - 1)
  span = csh.shape[1]
  for r in range(1, SUBLANES):
    csh[r - 1] = cbuf[r:r + span, :]
  for r0 in range(0, ts, CONV_ROWS):
    acc = jnp.broadcast_to(cb_ref[...], (CONV_ROWS, CONV_WIDTH))
    for j in range(CONV_KERNEL):
      r = (first + j) % SUBLANES
      lo = r0 + (first + j) - r
      win = cbuf[lo:lo + CONV_ROWS, :] if r == 0 else csh[r - 1, lo:lo + CONV_ROWS, :]
      acc = acc + cw_ref[j:j + 1, :] * win
    cvo[r0:r0 + CONV_ROWS, :] = acc

  cv = cvo[...]
  mu = jnp.mean(cv, axis=-1, keepdims=True)
  cen = cv - mu
  var = jnp.mean(cen * cen, axis=-1, keepdims=True)
  ln = (cen * lax.rsqrt(var + EPS)) * lng_ref[...] + lnb_ref[...]
  br_a = _dot((ln * _sigmoid(ln)).astype(BF16), pa_ref[...])

  first = LRU_HALO - (LRU_CONV - 1)
  xr = jnp.broadcast_to(lb_ref[...], (ts, LRU_WIDTH))
  for j in range(LRU_CONV):
    xr = xr + lw_ref[j:j + 1, :] * lbuf[first + j:first + j + ts, :]
  ri = _dot(xr.astype(BF16), wri_ref[...]) + bri_ref[...]
  r = _sigmoid(ri[:, :LRU_WIDTH])
  i = _sigmoid(ri[:, LRU_WIDTH:])
  log_a = (-LRU_C * r) * _softplus(-lam_ref[...])
  a = jnp.exp(log_a)
  th = jnp.tanh(log_a)
  u = jnp.sqrt(-2.0 * th / (1.0 - th)) * (i * xr)
  a_run, h = _linear_scan(a, u)
  h = h + a_run * hcar[...]
  hcar[...] = h[ts - 1:ts, :]
  br_c = _dot((_gelu_tanh(y_ref[...]) * h).astype(BF16), pc_ref[...])

  o_ref[...] = ga_ref[...] * br_a + gc_ref[...] * br_c


def _local_call(vg, y, xin, gates, w):
  bsz, seq, _ = vg.shape
  d = D_MODEL
  ts = TS_LOCAL
  row = lambda width, col=0: pl.BlockSpec((None, ts, width), lambda b, t: (b, t, col))
  weights = [w["cw"], w["cb"], w["lng"], w["lnb"], w["pa"],
             w["lw"], w["lb"], w["wri"], w["bri"], w["lam"], w["pc"]]
  return pl.pallas_call(
      _local_kernel,
      out_shape=jax.ShapeDtypeStruct((bsz, seq, d), F32),
      grid=(bsz, seq // ts),
      in_specs=[row(CONV_WIDTH), row(LRU_WIDTH), row(LRU_WIDTH), row(d, 0), row(d, 2)]
      + [_resident(a.shape) for a in weights],
      out_specs=row(d),
      scratch_shapes=[
          pltpu.VMEM((CONV_HALO + ts, CONV_WIDTH), F32),
          pltpu.VMEM((SUBLANES - 1, CONV_HALO + ts - SUBLANES, CONV_WIDTH), F32),
          pltpu.VMEM((ts, CONV_WIDTH), F32),
          pltpu.VMEM((LRU_HALO + ts, LRU_WIDTH), F32),
          pltpu.VMEM((1, LRU_WIDTH), F32),
      ],
      compiler_params=_params("arbitrary", "arbitrary"),
      name="local",
  )(vg, y, xin, gates, gates, *weights)


NEG_BIG = -1e30


def _attn_kernel(q_ref, k_ref, v_ref, o_ref, vaug_ref, m_ref, acc_ref):
  seq = q_ref.shape[0]
  v2 = v_ref[...].astype(F32)
  lane_v = lax.broadcasted_iota(jnp.int32, v2.shape, 1)
  vaug_ref[0] = jnp.where(lane_v < FOX_HEAD_DIM, v2,
                          jnp.where(lane_v == FOX_HEAD_DIM, 1.0, 0.0)).astype(BF16)
  vaug_ref[1] = jnp.where(lane_v >= FOX_HEAD_DIM, v2,
                          jnp.where(lane_v == 0, 1.0, 0.0)).astype(BF16)

  tq = m_ref.shape[1]

  def update(hh, q0, k0, r0, diagonal):
    rows = tq - r0
    q = q_ref[pl.ds(q0 + r0, rows), hh * HEAD_SLOT:(hh + 1) * HEAD_SLOT]
    k = k_ref[pl.ds(k0, TK), hh * HEAD_SLOT:(hh + 1) * HEAD_SLOT]
    s = lax.dot_general(q, k, (((1,), (1,)), ((), ())), preferred_element_type=F32)
    if diagonal:
      row = lax.broadcasted_iota(jnp.int32, (TK, TK), 0)
      col = lax.broadcasted_iota(jnp.int32, (TK, TK), 1)
      top = jnp.where(row >= col, s[:TK], NEG_BIG)
      s = top if rows == TK else jnp.concatenate([top, s[TK:]], axis=0)
    m_prev = m_ref[hh, r0:, :]
    m_next = jnp.maximum(m_prev, jnp.max(s, axis=1, keepdims=True))
    p = jnp.exp2(s - pltpu.repeat(m_next, TK // LANES, axis=1))
    alpha = jnp.exp2(m_prev - m_next)
    pv = _dot(p.astype(BF16), vaug_ref[hh, pl.ds(k0, TK), :])
    acc_ref[hh, r0:, :] = alpha * acc_ref[hh, r0:, :] + pv
    m_ref[hh, r0:, :] = m_next

  def q_tile(i, carry):
    q0 = pl.multiple_of(i * tq, tq)
    m_ref[...] = jnp.full(m_ref.shape, NEG_BIG, F32)
    acc_ref[...] = jnp.zeros_like(acc_ref)

    def kv_step(j, c):
      k0 = pl.multiple_of(j * TK, TK)
      for hh in range(2):
        update(hh, q0, k0, 0, False)
      return c

    lax.fori_loop(0, i * (tq // TK), kv_step, 0)
    for r0 in range(0, tq, TK):
      for hh in range(2):
        update(hh, q0, pl.multiple_of(q0 + r0, TK), r0, True)
    a0 = acc_ref[0]
    a1 = acc_ref[1]
    lane = lax.broadcasted_iota(jnp.int32, a0.shape, 1)
    o0 = a0 / a0[:, FOX_HEAD_DIM:FOX_HEAD_DIM + 1]
    o1 = a1 / a1[:, 0:1]
    o_ref[pl.ds(q0, tq), :] = jnp.where(lane < FOX_HEAD_DIM, o0, o1).astype(BF16)
    return carry

  lax.fori_loop(0, seq // tq, q_tile, 0)


def _attn_call(q, k, v):
  bsz, seq, _ = q.shape
  tq = min(TQ, seq)
  assert tq % TK == 0 and seq % tq == 0
  pair = lambda width: pl.BlockSpec((None, seq, width), lambda b, h: (b, 0, h))
  return pl.pallas_call(
      _attn_kernel,
      out_shape=jax.ShapeDtypeStruct((bsz, seq, FOX_WIDTH), BF16),
      grid=(bsz, FOX_HEADS // 2),
      in_specs=[pair(2 * HEAD_SLOT), pair(2 * HEAD_SLOT), pair(2 * FOX_HEAD_DIM)],
      out_specs=pair(2 * FOX_HEAD_DIM),
      scratch_shapes=[
          pltpu.VMEM((2, seq, 2 * FOX_HEAD_DIM), BF16),
          pltpu.VMEM((2, tq, LANES), F32),
          pltpu.VMEM((2, tq, 2 * FOX_HEAD_DIM), F32),
      ],
      compiler_params=_params("arbitrary", "arbitrary"),
      name="attn",
  )(q, k, v)


def _mlp_kernel(x_ref, o_ref, gb_ref, mac_ref, gt1_ref, sc_ref, sh_ref, gt2_ref, g_ref,
                pb_ref, wo_ref, w1_ref, w2_ref, out_ref):
  br_b = _dot(o_ref[...], pb_ref[...])
  merged = mac_ref[...] + gb_ref[...] * br_b
  x1 = x_ref[...] + gt1_ref[...] * _dot(merged.astype(BF16), wo_ref[...])
  hb = _modulated_norm(x1, g_ref[...], sc_ref[...], sh_ref[...]).astype(BF16)
  ffn = None
  for c0, cn in FFN_CHUNKS:
    a = _dot(hb, w1_ref[:, c0:c0 + cn])
    b = _dot(hb, w1_ref[:, FFN_HIDDEN + c0:FFN_HIDDEN + c0 + cn])
    part = _dot(((a * _sigmoid(a)) * b).astype(BF16), w2_ref[c0:c0 + cn, :])
    ffn = part if ffn is None else ffn + part
  out_ref[...] = x1 + gt2_ref[...] * ffn


def _mlp_call(x, o, gates, mac, gt1, sc2, sh2, gt2, g, w):
  bsz, seq, d = x.shape
  ts = TS_MLP
  row = lambda width, col=0: pl.BlockSpec((None, ts, width), lambda b, t: (b, t, col))
  per_batch = pl.BlockSpec((None, 1, d), lambda b, t: (b, 0, 0))
  weights = [w["pb"], w["wo"], w["w1"], w["w2"]]
  return pl.pallas_call(
      _mlp_kernel,
      out_shape=jax.ShapeDtypeStruct((bsz, seq, d), F32),
      grid=(bsz, seq // ts),
      in_specs=[row(d), row(FOX_WIDTH), row(d, 1), row(d),
                per_batch, per_batch, per_batch, per_batch, _resident((1, d))]
      + [_resident(a.shape) for a in weights],
      out_specs=row(d),
      compiler_params=_params("arbitrary", "arbitrary"),
      name="mlp",
  )(x, o, gates, mac, gt1, sc2, sh2, gt2, g, *weights)


def _pad_heads(w):
  d = w.shape[0]
  w = w.reshape(d, FOX_HEADS, FOX_HEAD_DIM)
  w = jnp.pad(w, ((0, 0), (0, 0), (0, HEAD_SLOT - FOX_HEAD_DIM)))
  return w.reshape(d, QK_WIDTH)


def _block_diag(w):
  n, a, b = w.shape
  eye = jnp.eye(n, dtype=w.dtype)
  return (eye[:, None, :, None] * w[:, :, None, :]).reshape(n * a, n * b)


def _placement_consts(ts):
  plq = np.zeros((LANES, QK_WIDTH), np.float32)
  plk = np.zeros((LANES, QK_WIDTH), np.float32)
  one_lane = FOX_HEADS * CUM_PARTS
  for hh in range(FOX_HEADS):
    base = hh * HEAD_SLOT + FOX_HEAD_DIM
    for i in range(CUM_PARTS):
      plq[FOX_HEADS * i + hh, base + i] = 1.0
      plq[one_lane, base + CUM_PARTS + i] = 1.0
      plk[one_lane, base + i] = 1.0
      plk[FOX_HEADS * i + hh, base + CUM_PARTS + i] = -1.0
  tri = np.tril(np.ones((ts, ts), np.float32))
  return {"plq": jnp.asarray(plq, BF16), "plk": jnp.asarray(plk, BF16),
          "tri": jnp.asarray(tri, BF16)}


def _layer_weights(l, p):
  d = D_MODEL
  w_in = p["w_in"][l]
  o_glu = 0
  o_q = o_glu + 2 * CONV_WIDTH
  o_k = o_q + FOX_WIDTH
  o_v = o_k + FOX_WIDTH
  o_f = o_v + FOX_WIDTH
  o_y = o_f + FOX_HEADS
  o_g = o_y + 2 * LRU_WIDTH
  wf = w_in[:, o_f:o_y]
  wf = jnp.pad(jnp.tile(wf, (1, CUM_PARTS)), ((0, 0), (0, LANES - CUM_PARTS * FOX_HEADS)))
  bf = jnp.pad(jnp.tile(p["fox_b_f"][l], CUM_PARTS), (0, LANES - CUM_PARTS * FOX_HEADS))
  scale = FOX_HEAD_DIM ** -0.5 * LOG2E
  gq = _pad_heads(jnp.tile(p["fox_q_norm_g"][l] * scale, FOX_HEADS)[None, :])
  gk = _pad_heads(jnp.tile(p["fox_k_norm_g"][l], FOX_HEADS)[None, :])
  row = lambda v: v[None, :]
  return {
      "glu": w_in[:, o_glu:o_q].astype(BF16),
      "q": _pad_heads(w_in[:, o_q:o_k]).astype(BF16),
      "k": _pad_heads(w_in[:, o_k:o_v]).astype(BF16),
      "v": w_in[:, o_v:o_f].astype(BF16),
      "f": wf.astype(BF16),
      "yx": w_in[:, o_y:o_g].astype(BF16),
      "gate": w_in[:, o_g:].astype(BF16),
      "bf": row(bf), "gq": gq, "gk": gk,
      "cw": p["conv_dw_w"][l], "cb": row(p["conv_dw_b"][l]),
      "lng": row(p["conv_ln_g"][l]), "lnb": row(p["conv_ln_b"][l]),
      "pa": p["w_conv_out"][l].astype(BF16),
      "lw": p["lru_conv_w"][l], "lb": row(p["lru_conv_b"][l]),
      "wri": jnp.concatenate([_block_diag(p["lru_w_r"][l]), _block_diag(p["lru_w_i"][l])],
                             axis=1).astype(BF16),
      "bri": row(jnp.concatenate([p["lru_b_r"][l], p["lru_b_i"][l]])),
      "lam": row(p["lru_lambda"][l]),
      "pc": p["w_lru_out"][l].astype(BF16),
      "pb": p["w_fox_out"][l].astype(BF16),
      "wo": p["w_o"][l].astype(BF16),
      "w1": p["w_ffn_in"][l].astype(BF16),
      "w2": p["w_ffn_out"][l].astype(BF16),
  }


def kernel(x, c, w_ada, b_ada, g_norm_mix, g_norm_ffn, w_in, conv_dw_w, conv_dw_b, conv_ln_g, conv_ln_b, w_conv_out, fox_b_f, fox_q_norm_g, fox_k_norm_g, w_fox_out, lru_conv_w, lru_conv_b, lru_w_r, lru_b_r, lru_w_i, lru_b_i, lru_lambda, w_lru_out, w_o, w_ffn_in, w_ffn_out):
  p = dict(w_in=w_in, conv_dw_w=conv_dw_w, conv_dw_b=conv_dw_b, conv_ln_g=conv_ln_g,
           conv_ln_b=conv_ln_b, w_conv_out=w_conv_out, fox_b_f=fox_b_f,
           fox_q_norm_g=fox_q_norm_g, fox_k_norm_g=fox_k_norm_g, w_fox_out=w_fox_out,
           lru_conv_w=lru_conv_w, lru_conv_b=lru_conv_b, lru_w_r=lru_w_r, lru_b_r=lru_b_r,
           lru_w_i=lru_w_i, lru_b_i=lru_b_i, lru_lambda=lru_lambda, w_lru_out=w_lru_out,
           w_o=w_o, w_ffn_in=w_ffn_in, w_ffn_out=w_ffn_out)
  bsz, seq, d = x.shape
  depth = w_ada.shape[0]
  assert d == D_MODEL and seq % max(TS_PROJ, TS_LOCAL, TS_MLP) == 0
  consts = _placement_consts(TS_PROJ)
  mod = _ada_call(c, w_ada, b_ada)
  for l in range(depth):
    sh1, sc1, gt1, sh2, sc2, gt2 = [m[:, None, :] for m in jnp.split(mod[l], 6, axis=-1)]
    w = _layer_weights(l, p)
    vg, q, k, v, y, xin, gates = _inproj_call(x, sc1, sh1, g_norm_mix[l][None, :], w, consts)
    mac = _local_call(vg, y, xin, gates, w)
    o = _attn_call(q, k, v)
    x = _mlp_call(x, o, gates, mac, gt1, sc2, sh2, gt2, g_norm_ffn[l][None, :], w)
  return x
```

```python
import jax
import jax.numpy as jnp
import numpy as np
from jax import lax
from jax.experimental import pallas as pl
from jax.experimental.pallas import tpu as pltpu

F32 = jnp.float32
BF16 = jnp.bfloat16

D_MODEL = 1024
EPS = 1e-6
CONV_WIDTH = 512
CONV_KERNEL = 31
FOX_HEADS = 8
FOX_HEAD_DIM = 64
FOX_WIDTH = FOX_HEADS * FOX_HEAD_DIM
LRU_WIDTH = 512
LRU_BLOCKS = 8
LRU_CONV = 4
LRU_C = 8.0
N_BRANCH = 3
FFN_HIDDEN = 2816

LANES = 128
SUBLANES = 8
HEAD_SLOT = LANES
QK_WIDTH = FOX_HEADS * HEAD_SLOT
CUM_PARTS = 3
CONV_HALO = 32
LRU_HALO = SUBLANES
VMEM_LIMIT = 56 * 1024 * 1024
LOG2E = float(np.log2(np.e))
NEG_BIG = -1e30

TS_PROJ = 256
TS_LOCAL = 256
TQ = 2048
TK = 512
TS_MLP = 512
CONV_ROWS = 64
FFN_CHUNKS = ((0, 1024), (1024, 1024), (2048, 768))


def _dot(a, b):
  return jnp.dot(a, b, preferred_element_type=F32)


def _sigmoid(x):
  return 1.0 / (1.0 + jnp.exp(-x))


def _softplus(x):
  return jnp.maximum(x, 0.0) + jnp.log1p(jnp.exp(-jnp.abs(x)))


def _split_bf16(x, parts):
  out = []
  rem = x
  for _ in range(parts):
    p = rem.astype(BF16)
    out.append(p)
    rem = rem - p.astype(F32)
  return out


def _resident(shape):
  nd = len(shape)
  return pl.BlockSpec(shape, lambda *_: (0,) * nd, pipeline_mode=pl.Buffered(1))


def _layer_block(stacked, l):
  shape = stacked.shape[1:]
  nd = len(shape)
  return pl.BlockSpec((None,) + shape, lambda *_: (l,) + (0,) * nd,
                      pipeline_mode=pl.Buffered(1))


def _params(*sem):
  return pltpu.CompilerParams(dimension_semantics=sem, vmem_limit_bytes=VMEM_LIMIT)


def _ada_kernel(c_ref, w_ref, b_ref, o_ref):
  c = c_ref[...]
  act = (c * _sigmoid(c)).astype(BF16)
  o_ref[...] = _dot(act, w_ref[...].astype(BF16)) + b_ref[...]


def _ada_call(c, w_ada, b_ada):
  depth, d, n = w_ada.shape
  bsz = c.shape[0]
  tn = d
  return pl.pallas_call(
      _ada_kernel,
      out_shape=jax.ShapeDtypeStruct((depth, bsz, n), F32),
      grid=(depth, n // tn),
      in_specs=[
          pl.BlockSpec((bsz, d), lambda l, j: (0, 0)),
          pl.BlockSpec((None, d, tn), lambda l, j: (l, 0, j)),
          pl.BlockSpec((None, 1, tn), lambda l, j: (l, 0, j)),
      ],
      out_specs=pl.BlockSpec((None, bsz, tn), lambda l, j: (l, 0, j)),
      compiler_params=_params("arbitrary", "arbitrary"),
      name="ada",
  )(c, w_ada, b_ada.reshape(depth, 1, n))


def _modulated_norm(x, g, sc, sh):
  ms = jnp.mean(x * x, axis=-1, keepdims=True)
  h = (x * lax.rsqrt(ms + EPS)) * g
  return h * (1.0 + sc) + sh


def _head_norm(z):
  parts = []
  for hh in range(FOX_HEADS):
    s = z[:, hh * HEAD_SLOT:(hh + 1) * HEAD_SLOT]
    ssq = jnp.sum(s * s, axis=-1, keepdims=True)
    parts.append(s * lax.rsqrt(ssq * (1.0 / FOX_HEAD_DIM) + EPS))
  return jnp.concatenate(parts, axis=-1)


def _inproj_kernel(x_ref, sc_ref, sh_ref, g_ref,
                   wglu_ref, wq_ref, wk_ref, wv_ref, wf_ref, wyx_ref, wg_ref,
                   bf_ref, gq_ref, gk_ref, tri_ref, plq_ref, plk_ref,
                   vg_ref, q_ref, k_ref, v_ref, y_ref, xin_ref, gate_ref,
                   carry_ref):
  ts = x_ref.shape[0]
  t = pl.program_id(1)
  hb = _modulated_norm(x_ref[...], g_ref[...], sc_ref[...], sh_ref[...]).astype(BF16)

  zg = _dot(hb, wglu_ref[...])
  vg_ref[...] = zg[:, :CONV_WIDTH] * _sigmoid(zg[:, CONV_WIDTH:])

  zf = _dot(hb, wf_ref[...]) + bf_ref[...]
  lf = jnp.minimum(zf, 0.0) - jnp.log1p(jnp.exp(-jnp.abs(zf)))

  @pl.when(t == 0)
  def _():
    carry_ref[...] = jnp.zeros_like(carry_ref)

  tri = tri_ref[...]
  cum = carry_ref[...]
  for piece in _split_bf16(lf, CUM_PARTS):
    cum = cum + _dot(tri, piece)
  carry_ref[...] = cum[ts - 1:ts, :]

  lane = lax.broadcasted_iota(jnp.int32, cum.shape, 1)
  pieces = _split_bf16(cum * LOG2E, CUM_PARTS)
  src = jnp.where(lane == FOX_HEADS * CUM_PARTS, 1.0, 0.0)
  for i in reversed(range(CUM_PARTS)):
    src = jnp.where(lane < FOX_HEADS * (i + 1), pieces[i].astype(F32), src)
  src = src.astype(BF16)

  qn = _head_norm(_dot(hb, wq_ref[...])) * gq_ref[...]
  q_ref[...] = (qn + _dot(src, plq_ref[...])).astype(BF16)
  kn = _head_norm(_dot(hb, wk_ref[...])) * gk_ref[...]
  k_ref[...] = (kn + _dot(src, plk_ref[...])).astype(BF16)

  v_ref[...] = _dot(hb, wv_ref[...]).astype(BF16)

  zyx = _dot(hb, wyx_ref[...])
  y_ref[...] = zyx[:, :LRU_WIDTH]
  xin_ref[...] = zyx[:, LRU_WIDTH:]

  gate_ref[...] = _sigmoid(_dot(hb, wg_ref[...]))


def _inproj_call(l, x, sc, sh, w, consts):
  bsz, seq, d = x.shape
  ts = TS_PROJ
  grid = (bsz, seq // ts)
  row = lambda width: pl.BlockSpec((None, ts, width), lambda b, t: (b, t, 0))
  per_batch = pl.BlockSpec((None, 1, d), lambda b, t: (b, 0, 0))
  weights = [w["g_mix"], w["glu"], w["q"], w["k"], w["v"], w["f"], w["yx"], w["gate"],
             w["bf"], w["gq"], w["gk"]]
  shared = [consts["tri"], consts["plq"], consts["plk"]]
  out_widths = [(CONV_WIDTH, F32), (QK_WIDTH, BF16), (QK_WIDTH, BF16), (FOX_WIDTH, BF16),
                (LRU_WIDTH, F32), (LRU_WIDTH, F32), (N_BRANCH * d, F32)]
  return pl.pallas_call(
      _inproj_kernel,
      out_shape=[jax.ShapeDtypeStruct((bsz, seq, wd), dt) for wd, dt in out_widths],
      grid=grid,
      in_specs=[row(d), per_batch, per_batch]
      + [_layer_block(a, l) for a in weights] + [_resident(a.shape) for a in shared],
      out_specs=[row(wd) for wd, _ in out_widths],
      scratch_shapes=[pltpu.VMEM((1, LANES), F32)],
      compiler_params=_params("arbitrary", "arbitrary"),
      name="in_proj",
  )(x, sc, sh, *weights, *shared)


def _gelu_tanh(x):
  c = np.float32(np.sqrt(2.0 / np.pi))
  return 0.5 * x * (1.0 + jnp.tanh(c * (x + 0.044715 * (x * x * x))))


def _shift_up(x, r):
  rows, width = x.shape
  x3 = x.reshape(rows // SUBLANES, SUBLANES, width)
  rot = pltpu.roll(x3, SUBLANES - r, 1)
  sub = lax.broadcasted_iota(jnp.int32, (rows // SUBLANES - 1, SUBLANES, width), 1)
  out = jnp.where(sub < SUBLANES - r, rot[:-1], rot[1:])
  return out.reshape(rows - SUBLANES, width)


def _linear_scan(a, u, carry):
  ts, width = a.shape
  a = a.reshape(ts // SUBLANES, SUBLANES, width)
  u = u.reshape(ts // SUBLANES, SUBLANES, width)
  sub = lax.broadcasted_iota(jnp.int32, a.shape, 1)
  d = 1
  while d < SUBLANES:
    keep = sub >= d
    u = u + a * jnp.where(keep, pltpu.roll(u, d, 1), 0.0)
    a = a * jnp.where(keep, pltpu.roll(a, d, 1), 1.0)
    d *= 2
  groups = []
  for g in range(ts // SUBLANES):
    h = u[g] + a[g] * carry
    groups.append(h)
    carry = h[SUBLANES - 1:SUBLANES]
  return jnp.concatenate(groups, axis=0), carry


def _local_kernel(vg_ref, y_ref, xin_ref, ga_ref, gc_ref,
                  cw_ref, cb_ref, lng_ref, lnb_ref, pa_ref,
                  lw_ref, lb_ref, wri_ref, bri_ref, lam_ref, pc_ref,
                  o_ref, cbuf, csh, cvo, lbuf, hcar):
  ts = vg_ref.shape[0]
  t = pl.program_id(1)

  @pl.when(t == 0)
  def _():
    cbuf[0:CONV_HALO, :] = jnp.zeros((CONV_HALO, CONV_WIDTH), F32)
    lbuf[0:LRU_HALO, :] = jnp.zeros((LRU_HALO, LRU_WIDTH), F32)
    hcar[...] = jnp.zeros_like(hcar)

  @pl.when(t > 0)
  def _():
    cbuf[0:CONV_HALO, :] = cbuf[ts:ts + CONV_HALO, :]
    lbuf[0:LRU_HALO, :] = lbuf[ts:ts + LRU_HALO, :]

  cbuf[CONV_HALO:CONV_HALO + ts, :] = vg_ref[...]
  lbuf[LRU_HALO:LRU_HALO + ts, :] = xin_ref[...]

  first = CONV_HALO - (CONV_KERNEL - 1)
  for r in range(1, SUBLANES):
    csh[r - 1] = _shift_up(cbuf[...], r)
  for r0 in range(0, ts, CONV_ROWS):
    acc = jnp.broadcast_to(cb_ref[...], (CONV_ROWS, CONV_WIDTH))
    for j in range(CONV_KERNEL):
      r = (first + j) % SUBLANES
      lo = r0 + (first + j) - r
      win = cbuf[lo:lo + CONV_ROWS, :] if r == 0 else csh[r - 1, lo:lo + CONV_ROWS, :]
      acc = acc + cw_ref[j:j + 1, :] * win
    cvo[r0:r0 + CONV_ROWS, :] = acc

  cv = cvo[...]
  mu = jnp.mean(cv, axis=-1, keepdims=True)
  cen = cv - mu
  var = jnp.mean(cen * cen, axis=-1, keepdims=True)
  ln = (cen * lax.rsqrt(var + EPS)) * lng_ref[...] + lnb_ref[...]
  br_a = _dot((ln * _sigmoid(ln)).astype(BF16), pa_ref[...])

  first = LRU_HALO - (LRU_CONV - 1)
  xr = jnp.broadcast_to(lb_ref[...], (ts, LRU_WIDTH))
  xin = lbuf[...]
  for j in range(LRU_CONV):
    off = first + j
    win = xin[off:, :] if off % SUBLANES == 0 else _shift_up(xin, off)
    xr = xr + lw_ref[j:j + 1, :] * win[:ts]
  ri = _dot(xr.astype(BF16), wri_ref[...]) + bri_ref[...]
  r = _sigmoid(ri[:, :LRU_WIDTH])
  i = _sigmoid(ri[:, LRU_WIDTH:])
  log_a = (-LRU_C * r) * _softplus(-lam_ref[...])
  a = jnp.exp(log_a)
  th = jnp.tanh(log_a)
  u = jnp.sqrt(-2.0 * th / (1.0 - th)) * (i * xr)
  h, hcar[...] = _linear_scan(a, u, hcar[...])
  br_c = _dot((_gelu_tanh(y_ref[...]) * h).astype(BF16), pc_ref[...])

  o_ref[...] = ga_ref[...] * br_a + gc_ref[...] * br_c


def _local_call(l, vg, y, xin, gates, w):
  bsz, seq, _ = vg.shape
  d = D_MODEL
  ts = TS_LOCAL
  row = lambda width, col=0: pl.BlockSpec((None, ts, width), lambda b, t: (b, t, col))
  weights = [w["cw"], w["cb"], w["lng"], w["lnb"], w["pa"],
             w["lw"], w["lb"], w["wri"], w["bri"], w["lam"], w["pc"]]
  return pl.pallas_call(
      _local_kernel,
      out_shape=jax.ShapeDtypeStruct((bsz, seq, d), F32),
      grid=(bsz, seq // ts),
      in_specs=[row(CONV_WIDTH), row(LRU_WIDTH), row(LRU_WIDTH), row(d, 0), row(d, 2)]
      + [_layer_block(a, l) for a in weights],
      out_specs=row(d),
      scratch_shapes=[
          pltpu.VMEM((CONV_HALO + ts, CONV_WIDTH), F32),
          pltpu.VMEM((SUBLANES - 1, CONV_HALO + ts - SUBLANES, CONV_WIDTH), F32),
          pltpu.VMEM((ts, CONV_WIDTH), F32),
          pltpu.VMEM((LRU_HALO + ts, LRU_WIDTH), F32),
          pltpu.VMEM((1, LRU_WIDTH), F32),
      ],
      compiler_params=_params("arbitrary", "arbitrary"),
      name="local",
  )(vg, y, xin, gates, gates, *weights)


def _attn_kernel(q_ref, k_ref, v_ref, o_ref, vaug_ref, m_ref, acc_ref):
  seq = q_ref.shape[0]
  v2 = v_ref[...].astype(F32)
  lane_v = lax.broadcasted_iota(jnp.int32, v2.shape, 1)
  vaug_ref[0] = jnp.where(lane_v < FOX_HEAD_DIM, v2,
                          jnp.where(lane_v == FOX_HEAD_DIM, 1.0, 0.0)).astype(BF16)
  vaug_ref[1] = jnp.where(lane_v >= FOX_HEAD_DIM, v2,
                          jnp.where(lane_v == 0, 1.0, 0.0)).astype(BF16)
  tq = m_ref.shape[1]

  def update(hh, q0, k0, r0, diagonal):
    rows = tq - r0
    q = q_ref[pl.ds(q0 + r0, rows), hh * HEAD_SLOT:(hh + 1) * HEAD_SLOT]
    k = k_ref[pl.ds(k0, TK), hh * HEAD_SLOT:(hh + 1) * HEAD_SLOT]
    s = lax.dot_general(q, k, (((1,), (1,)), ((), ())), preferred_element_type=F32)
    if diagonal:
      row = lax.broadcasted_iota(jnp.int32, (TK, TK), 0)
      col = lax.broadcasted_iota(jnp.int32, (TK, TK), 1)
      top = jnp.where(row >= col, s[:TK], NEG_BIG)
      s = top if rows == TK else jnp.concatenate([top, s[TK:]], axis=0)
    m_prev = m_ref[hh, r0:, :]
    m_next = jnp.maximum(m_prev, jnp.max(s, axis=1, keepdims=True))
    p = jnp.exp2(s - jnp.concatenate([m_next] * (TK // LANES), axis=1))
    alpha = jnp.exp2(m_prev - m_next)
    pv = _dot(p.astype(BF16), vaug_ref[hh, pl.ds(k0, TK), :])
    acc_ref[hh, r0:, :] = alpha * acc_ref[hh, r0:, :] + pv
    m_ref[hh, r0:, :] = m_next

  def q_tile(i, carry):
    q0 = pl.multiple_of(i * tq, tq)
    m_ref[...] = jnp.full(m_ref.shape, NEG_BIG, F32)
    acc_ref[...] = jnp.zeros_like(acc_ref)

    def kv_step(j, c):
      k0 = pl.multiple_of(j * TK, TK)
      for hh in range(2):
        update(hh, q0, k0, 0, False)
      return c

    lax.fori_loop(0, i * (tq // TK), kv_step, 0)
    for r0 in range(0, tq, TK):
      for hh in range(2):
        update(hh, q0, pl.multiple_of(q0 + r0, TK), r0, True)
    a0 = acc_ref[0]
    a1 = acc_ref[1]
    lane = lax.broadcasted_iota(jnp.int32, a0.shape, 1)
    o0 = a0 / a0[:, FOX_HEAD_DIM:FOX_HEAD_DIM + 1]
    o1 = a1 / a1[:, 0:1]
    o_ref[pl.ds(q0, tq), :] = jnp.where(lane < FOX_HEAD_DIM, o0, o1).astype(BF16)
    return carry

  lax.fori_loop(0, seq // tq, q_tile, 0)


def _attn_call(q, k, v):
  bsz, seq, _ = q.shape
  tq = min(TQ, seq)
  assert tq % TK == 0 and seq % tq == 0
  pair = lambda width: pl.BlockSpec((None, seq, width), lambda b, h: (b, 0, h))
  return pl.pallas_call(
      _attn_kernel,
      out_shape=jax.ShapeDtypeStruct((bsz, seq, FOX_WIDTH), BF16),
      grid=(bsz, FOX_HEADS // 2),
      in_specs=[pair(2 * HEAD_SLOT), pair(2 * HEAD_SLOT), pair(2 * FOX_HEAD_DIM)],
      out_specs=pair(2 * FOX_HEAD_DIM),
      scratch_shapes=[
          pltpu.VMEM((2, seq, 2 * FOX_HEAD_DIM), BF16),
          pltpu.VMEM((2, tq, LANES), F32),
          pltpu.VMEM((2, tq, 2 * FOX_HEAD_DIM), F32),
      ],
      compiler_params=_params("arbitrary", "arbitrary"),
      name="attn",
  )(q, k, v)


def _mlp_kernel(x_ref, o_ref, gb_ref, mac_ref, gt1_ref, sc_ref, sh_ref, gt2_ref, g_ref,
                pb_ref, wo_ref, w1_ref, w2_ref, out_ref):
  br_b = _dot(o_ref[...], pb_ref[...])
  merged = mac_ref[...] + gb_ref[...] * br_b
  x1 = x_ref[...] + gt1_ref[...] * _dot(merged.astype(BF16), wo_ref[...])
  hb = _modulated_norm(x1, g_ref[...], sc_ref[...], sh_ref[...]).astype(BF16)
  ffn = None
  for c0, cn in FFN_CHUNKS:
    a = _dot(hb, w1_ref[:, c0:c0 + cn])
    b = _dot(hb, w1_ref[:, FFN_HIDDEN + c0:FFN_HIDDEN + c0 + cn])
    part = _dot(((a * _sigmoid(a)) * b).astype(BF16), w2_ref[c0:c0 + cn, :])
    ffn = part if ffn is None else ffn + part
  out_ref[...] = x1 + gt2_ref[...] * ffn


def _mlp_call(l, x, o, gates, mac, gt1, sc2, sh2, gt2, w):
  bsz, seq, d = x.shape
  ts = TS_MLP
  row = lambda width, col=0: pl.BlockSpec((None, ts, width), lambda b, t: (b, t, col))
  per_batch = pl.BlockSpec((None, 1, d), lambda b, t: (b, 0, 0))
  weights = [w["g_ffn"], w["pb"], w["wo"], w["w1"], w["w2"]]
  return pl.pallas_call(
      _mlp_kernel,
      out_shape=jax.ShapeDtypeStruct((bsz, seq, d), F32),
      grid=(bsz, seq // ts),
      in_specs=[row(d), row(FOX_WIDTH), row(d, 1), row(d),
                per_batch, per_batch, per_batch, per_batch]
      + [_layer_block(a, l) for a in weights],
      out_specs=row(d),
      compiler_params=_params("arbitrary", "arbitrary"),
      name="mlp",
  )(x, o, gates, mac, gt1, sc2, sh2, gt2, *weights)


def _pad_heads(w):
  lead = w.shape[:-1]
  w = w.reshape(lead + (FOX_HEADS, FOX_HEAD_DIM))
  w = jnp.pad(w, [(0, 0)] * (len(lead) + 1) + [(0, HEAD_SLOT - FOX_HEAD_DIM)])
  return w.reshape(lead + (QK_WIDTH,))


def _block_diag(w):
  depth, n, a, b = w.shape
  eye = jnp.eye(n, dtype=w.dtype)
  return (eye[None, :, None, :, None] * w[:, :, :, None, :]).reshape(depth, n * a, n * b)


def _placement_consts(ts):
  plq = np.zeros((LANES, QK_WIDTH), np.float32)
  plk = np.zeros((LANES, QK_WIDTH), np.float32)
  one_lane = FOX_HEADS * CUM_PARTS
  for hh in range(FOX_HEADS):
    base = hh * HEAD_SLOT + FOX_HEAD_DIM
    for i in range(CUM_PARTS):
      plq[FOX_HEADS * i + hh, base + i] = 1.0
      plq[one_lane, base + CUM_PARTS + i] = 1.0
      plk[one_lane, base + i] = 1.0
      plk[FOX_HEADS * i + hh, base + CUM_PARTS + i] = -1.0
  tri = np.tril(np.ones((ts, ts), np.float32))
  return {"plq": jnp.asarray(plq, BF16), "plk": jnp.asarray(plk, BF16),
          "tri": jnp.asarray(tri, BF16)}


def _stacked_weights(p):
  w_in = p["w_in"]
  o_glu = 0
  o_q = o_glu + 2 * CONV_WIDTH
  o_k = o_q + FOX_WIDTH
  o_v = o_k + FOX_WIDTH
  o_f = o_v + FOX_WIDTH
  o_y = o_f + FOX_HEADS
  o_g = o_y + 2 * LRU_WIDTH
  pad_f = LANES - CUM_PARTS * FOX_HEADS
  wf = jnp.pad(jnp.tile(w_in[:, :, o_f:o_y], (1, 1, CUM_PARTS)), ((0, 0), (0, 0), (0, pad_f)))
  bf = jnp.pad(jnp.tile(p["fox_b_f"], (1, CUM_PARTS)), ((0, 0), (0, pad_f)))
  scale = FOX_HEAD_DIM ** -0.5 * LOG2E
  gq = _pad_heads(jnp.tile(p["fox_q_norm_g"] * scale, (1, FOX_HEADS)))
  gk = _pad_heads(jnp.tile(p["fox_k_norm_g"], (1, FOX_HEADS)))
  row = lambda v: v[:, None, :]
  return {
      "g_mix": row(p["g_norm_mix"]), "g_ffn": row(p["g_norm_ffn"]),
      "glu": w_in[:, :, o_glu:o_q].astype(BF16),
      "q": _pad_heads(w_in[:, :, o_q:o_k]).astype(BF16),
      "k": _pad_heads(w_in[:, :, o_k:o_v]).astype(BF16),
      "v": w_in[:, :, o_v:o_f].astype(BF16),
      "f": wf.astype(BF16),
      "yx": w_in[:, :, o_y:o_g].astype(BF16),
      "gate": w_in[:, :, o_g:].astype(BF16),
      "bf": row(bf), "gq": row(gq), "gk": row(gk),
      "cw": p["conv_dw_w"], "cb": row(p["conv_dw_b"]),
      "lng": row(p["conv_ln_g"]), "lnb": row(p["conv_ln_b"]),
      "pa": p["w_conv_out"].astype(BF16),
      "lw": p["lru_conv_w"], "lb": row(p["lru_conv_b"]),
      "wri": jnp.concatenate([_block_diag(p["lru_w_r"]), _block_diag(p["lru_w_i"])],
                             axis=2).astype(BF16),
      "bri": row(jnp.concatenate([p["lru_b_r"], p["lru_b_i"]], axis=1)),
      "lam": row(p["lru_lambda"]),
      "pc": p["w_lru_out"].astype(BF16),
      "pb": p["w_fox_out"].astype(BF16),
      "wo": p["w_o"].astype(BF16),
      "w1": p["w_ffn_in"].astype(BF16),
      "w2": p["w_ffn_out"].astype(BF16),
  }


def kernel(x, c, w_ada, b_ada, g_norm_mix, g_norm_ffn, w_in, conv_dw_w, conv_dw_b, conv_ln_g, conv_ln_b, w_conv_out, fox_b_f, fox_q_norm_g, fox_k_norm_g, w_fox_out, lru_conv_w, lru_conv_b, lru_w_r, lru_b_r, lru_w_i, lru_b_i, lru_lambda, w_lru_out, w_o, w_ffn_in, w_ffn_out):
  p = dict(g_norm_mix=g_norm_mix, g_norm_ffn=g_norm_ffn, w_in=w_in,
           conv_dw_w=conv_dw_w, conv_dw_b=conv_dw_b, conv_ln_g=conv_ln_g,
           conv_ln_b=conv_ln_b, w_conv_out=w_conv_out, fox_b_f=fox_b_f,
           fox_q_norm_g=fox_q_norm_g, fox_k_norm_g=fox_k_norm_g, w_fox_out=w_fox_out,
           lru_conv_w=lru_conv_w, lru_conv_b=lru_conv_b, lru_w_r=lru_w_r, lru_b_r=lru_b_r,
           lru_w_i=lru_w_i, lru_b_i=lru_b_i, lru_lambda=lru_lambda, w_lru_out=w_lru_out,
           w_o=w_o, w_ffn_in=w_ffn_in, w_ffn_out=w_ffn_out)
  bsz, seq, d = x.shape
  depth = w_ada.shape[0]
  assert d == D_MODEL and seq % max(TS_PROJ, TS_LOCAL, TS_MLP) == 0
  consts = _placement_consts(TS_PROJ)
  w = _stacked_weights(p)
  mod = _ada_call(c, w_ada, b_ada)
  for l in range(depth):
    sh1, sc1, gt1, sh2, sc2, gt2 = [m[:, None, :] for m in jnp.split(mod[l], 6, axis=-1)]
    vg, q, k, v, y, xin, gates = _inproj_call(l, x, sc1, sh1, w, consts)
    mac = _local_call(l, vg, y, xin, gates, w)
    o = _attn_call(q, k, v)
    x = _mlp_call(l, x, o, gates, mac, gt1, sc2, sh2, gt2, w)
  return x
```

```python
import jax
import jax.numpy as jnp
import numpy as np
from jax import lax
from jax.experimental import pallas as pl
from jax.experimental.pallas import tpu as pltpu

F32 = jnp.float32
BF16 = jnp.bfloat16

D_MODEL = 1024
EPS = 1e-6
CONV_WIDTH = 512
CONV_KERNEL = 31
FOX_HEADS = 8
FOX_HEAD_DIM = 64
FOX_WIDTH = FOX_HEADS * FOX_HEAD_DIM
LRU_WIDTH = 512
LRU_BLOCKS = 8
LRU_CONV = 4
LRU_C = 8.0
N_BRANCH = 3
FFN_HIDDEN = 2816

LANES = 128
SUBLANES = 8
HEAD_PAIRS = FOX_HEADS // 2
CUM_PARTS = 3
AUG_COLS = 2 * CUM_PARTS
CONV_HALO = 32
LRU_HALO = SUBLANES
VMEM_LIMIT = 56 * 1024 * 1024
LOG2E = float(np.log2(np.e))
NEG_BIG = -1e30

TS_PROJ = 256
TS_LOCAL = 256
TQ = 2048
TK = 512
TS_MLP = 512
CONV_ROWS = 64
FFN_CHUNKS = ((0, 1024), (1024, 1024), (2048, 768))


def _dot(a, b):
  return jnp.dot(a, b, preferred_element_type=F32)


def _sigmoid(x):
  return 1.0 / (1.0 + jnp.exp(-x))


def _softplus(x):
  return jnp.maximum(x, 0.0) + jnp.log1p(jnp.exp(-jnp.abs(x)))


def _split_bf16(x, parts):
  out = []
  rem = x
  for _ in range(parts):
    p = rem.astype(BF16)
    out.append(p)
    rem = rem - p.astype(F32)
  return out


def _resident(shape):
  nd = len(shape)
  return pl.BlockSpec(shape, lambda *_: (0,) * nd, pipeline_mode=pl.Buffered(1))


def _layer_block(stacked, l):
  shape = stacked.shape[1:]
  nd = len(shape)
  return pl.BlockSpec((None,) + shape, lambda *_: (l,) + (0,) * nd,
                      pipeline_mode=pl.Buffered(1))


def _layer_cols(stacked, l, width, idx):
  rows = stacked.shape[1]
  return pl.BlockSpec((None, rows, width), lambda *_: (l, 0, idx),
                      pipeline_mode=pl.Buffered(1))


def _params(*sem):
  return pltpu.CompilerParams(dimension_semantics=sem, vmem_limit_bytes=VMEM_LIMIT)


def _ada_kernel(c_ref, w_ref, b_ref, o_ref):
  c = c_ref[...]
  act = (c * _sigmoid(c)).astype(BF16)
  o_ref[...] = _dot(act, w_ref[...].astype(BF16)) + b_ref[...]


def _ada_call(c, w_ada, b_ada):
  depth, d, n = w_ada.shape
  bsz = c.shape[0]
  tn = d
  return pl.pallas_call(
      _ada_kernel,
      out_shape=jax.ShapeDtypeStruct((depth, bsz, n), F32),
      grid=(depth, n // tn),
      in_specs=[
          pl.BlockSpec((bsz, d), lambda l, j: (0, 0)),
          pl.BlockSpec((None, d, tn), lambda l, j: (l, 0, j)),
          pl.BlockSpec((None, 1, tn), lambda l, j: (l, 0, j)),
      ],
      out_specs=pl.BlockSpec((None, bsz, tn), lambda l, j: (l, 0, j)),
      compiler_params=_params("arbitrary", "arbitrary"),
      name="ada",
  )(c, w_ada, b_ada.reshape(depth, 1, n))


def _modulated_norm(x, g, sc, sh):
  ms = jnp.mean(x * x, axis=-1, keepdims=True)
  h = (x * lax.rsqrt(ms + EPS)) * g
  return h * (1.0 + sc) + sh


def _head_norm(z):
  parts = []
  for hp in range(HEAD_PAIRS):
    s = z[:, hp * LANES:(hp + 1) * LANES]
    low = lax.broadcasted_iota(jnp.int32, s.shape, 1) < FOX_HEAD_DIM
    sq = s * s
    ssq = jnp.where(low,
                    jnp.sum(jnp.where(low, sq, 0.0), axis=-1, keepdims=True),
                    jnp.sum(jnp.where(low, 0.0, sq), axis=-1, keepdims=True))
    parts.append(s * lax.rsqrt(ssq * (1.0 / FOX_HEAD_DIM) + EPS))
  return jnp.concatenate(parts, axis=-1)


def _inproj_kernel(x_ref, sc_ref, sh_ref, g_ref,
                   wglu_ref, wq_ref, wk_ref, wv_ref, wf_ref, wyx_ref, wg_ref,
                   bf_ref, gq_ref, gk_ref, tri_ref, plq_ref, plk_ref,
                   vg_ref, q_ref, k_ref, qa_ref, ka_ref, v_ref, y_ref, xin_ref, gate_ref,
                   carry_ref):
  ts = x_ref.shape[0]
  t = pl.program_id(1)
  hb = _modulated_norm(x_ref[...], g_ref[...], sc_ref[...], sh_ref[...]).astype(BF16)

  zg = _dot(hb, wglu_ref[...])
  vg_ref[...] = zg[:, :CONV_WIDTH] * _sigmoid(zg[:, CONV_WIDTH:])

  zf = _dot(hb, wf_ref[...]) + bf_ref[...]
  lf = jnp.minimum(zf, 0.0) - jnp.log1p(jnp.exp(-jnp.abs(zf)))

  @pl.when(t == 0)
  def _():
    carry_ref[...] = jnp.zeros_like(carry_ref)

  tri = tri_ref[...]
  cum = carry_ref[...]
  for piece in _split_bf16(lf, CUM_PARTS):
    cum = cum + _dot(tri, piece)
  carry_ref[...] = cum[ts - 1:ts, :]

  lane = lax.broadcasted_iota(jnp.int32, cum.shape, 1)
  pieces = _split_bf16(cum * LOG2E, CUM_PARTS)
  src = jnp.where(lane == FOX_HEADS * CUM_PARTS, 1.0, 0.0)
  for i in reversed(range(CUM_PARTS)):
    src = jnp.where(lane < FOX_HEADS * (i + 1), pieces[i].astype(F32), src)
  src = src.astype(BF16)

  q_ref[...] = (_head_norm(_dot(hb, wq_ref[...])) * gq_ref[...]).astype(BF16)
  k_ref[...] = (_head_norm(_dot(hb, wk_ref[...])) * gk_ref[...]).astype(BF16)
  qa_ref[...] = _dot(src, plq_ref[...]).astype(BF16)
  ka_ref[...] = _dot(src, plk_ref[...]).astype(BF16)

  v_ref[...] = _dot(hb, wv_ref[...]).astype(BF16)

  zyx = _dot(hb, wyx_ref[...])
  y_ref[...] = zyx[:, :LRU_WIDTH]
  xin_ref[...] = zyx[:, LRU_WIDTH:]

  gate_ref[...] = _sigmoid(_dot(hb, wg_ref[...]))


def _inproj_call(l, x, sc, sh, w, consts):
  bsz, seq, d = x.shape
  ts = TS_PROJ
  grid = (bsz, seq // ts)
  row = lambda width: pl.BlockSpec((None, ts, width), lambda b, t: (b, t, 0))
  per_batch = pl.BlockSpec((None, 1, d), lambda b, t: (b, 0, 0))
  head = w["in_head"]
  head_cols = [(2 * CONV_WIDTH, 0), (FOX_WIDTH, 2), (FOX_WIDTH, 3), (FOX_WIDTH, 4)]
  weights = [w["f"], w["yx"], w["gate"], w["bf"], w["gq"], w["gk"]]
  shared = [consts["tri"], consts["plq"], consts["plk"]]
  out_widths = [(CONV_WIDTH, F32), (FOX_WIDTH, BF16), (FOX_WIDTH, BF16),
                (HEAD_PAIRS * LANES, BF16), (HEAD_PAIRS * LANES, BF16), (FOX_WIDTH, BF16),
                (LRU_WIDTH, F32), (LRU_WIDTH, F32), (N_BRANCH * d, F32)]
  return pl.pallas_call(
      _inproj_kernel,
      out_shape=[jax.ShapeDtypeStruct((bsz, seq, wd), dt) for wd, dt in out_widths],
      grid=grid,
      in_specs=[row(d), per_batch, per_batch, _layer_block(w["g_mix"], l)]
      + [_layer_cols(head, l, width, idx) for width, idx in head_cols]
      + [_layer_block(a, l) for a in weights] + [_resident(a.shape) for a in shared],
      out_specs=[row(wd) for wd, _ in out_widths],
      scratch_shapes=[pltpu.VMEM((1, LANES), F32)],
      compiler_params=_params("arbitrary", "arbitrary"),
      name="in_proj",
  )(x, sc, sh, w["g_mix"], head, head, head, head, *weights, *shared)


def _gelu_tanh(x):
  c = np.float32(np.sqrt(2.0 / np.pi))
  return 0.5 * x * (1.0 + jnp.tanh(c * (x + 0.044715 * (x * x * x))))


def _shift_up(x, r):
  rows, width = x.shape
  x3 = x.reshape(rows // SUBLANES, SUBLANES, width)
  rot = pltpu.roll(x3, SUBLANES - r, 1)
  sub = lax.broadcasted_iota(jnp.int32, (rows // SUBLANES - 1, SUBLANES, width), 1)
  out = jnp.where(sub < SUBLANES - r, rot[:-1], rot[1:])
  return out.reshape(rows - SUBLANES, width)


def _linear_scan(a, u, carry):
  ts, width = a.shape
  a = a.reshape(ts // SUBLANES, SUBLANES, width)
  u = u.reshape(ts // SUBLANES, SUBLANES, width)
  sub = lax.broadcasted_iota(jnp.int32, a.shape, 1)
  d = 1
  while d < SUBLANES:
    keep = sub >= d
    u = u + a * jnp.where(keep, pltpu.roll(u, d, 1), 0.0)
    a = a * jnp.where(keep, pltpu.roll(a, d, 1), 1.0)
    d *= 2
  groups = []
  for g in range(ts // SUBLANES):
    h = u[g] + a[g] * carry
    groups.append(h)
    carry = h[SUBLANES - 1:SUBLANES]
  return jnp.concatenate(groups, axis=0), carry


def _local_kernel(vg_ref, y_ref, xin_ref, ga_ref, gc_ref,
                  cw_ref, cb_ref, lng_ref, lnb_ref, pa_ref,
                  lw_ref, lb_ref, wri_ref, bri_ref, lam_ref, pc_ref,
                  o_ref, cbuf, csh, cvo, lbuf, hcar):
  ts = vg_ref.shape[0]
  t = pl.program_id(1)

  @pl.when(t == 0)
  def _():
    cbuf[0:CONV_HALO, :] = jnp.zeros((CONV_HALO, CONV_WIDTH), F32)
    lbuf[0:LRU_HALO, :] = jnp.zeros((LRU_HALO, LRU_WIDTH), F32)
    hcar[...] = jnp.zeros_like(hcar)

  @pl.when(t > 0)
  def _():
    cbuf[0:CONV_HALO, :] = cbuf[ts:ts + CONV_HALO, :]
    lbuf[0:LRU_HALO, :] = lbuf[ts:ts + LRU_HALO, :]

  cbuf[CONV_HALO:CONV_HALO + ts, :] = vg_ref[...]
  lbuf[LRU_HALO:LRU_HALO + ts, :] = xin_ref[...]

  first = CONV_HALO - (CONV_KERNEL - 1)
  for r in range(1, SUBLANES):
    csh[r - 1] = _shift_up(cbuf[...], r)
  for r0 in range(0, ts, CONV_ROWS):
    acc = jnp.broadcast_to(cb_ref[...], (CONV_ROWS, CONV_WIDTH))
    for j in range(CONV_KERNEL):
      r = (first + j) % SUBLANES
      lo = r0 + (first + j) - r
      win = cbuf[lo:lo + CONV_ROWS, :] if r == 0 else csh[r - 1, lo:lo + CONV_ROWS, :]
      acc = acc + cw_ref[j:j + 1, :] * win
    cvo[r0:r0 + CONV_ROWS, :] = acc

  cv = cvo[...]
  mu = jnp.mean(cv, axis=-1, keepdims=True)
  cen = cv - mu
  var = jnp.mean(cen * cen, axis=-1, keepdims=True)
  ln = (cen * lax.rsqrt(var + EPS)) * lng_ref[...] + lnb_ref[...]
  br_a = _dot((ln * _sigmoid(ln)).astype(BF16), pa_ref[...])

  first = LRU_HALO - (LRU_CONV - 1)
  xr = jnp.broadcast_to(lb_ref[...], (ts, LRU_WIDTH))
  xin = lbuf[...]
  for j in range(LRU_CONV):
    off = first + j
    win = xin[off:, :] if off % SUBLANES == 0 else _shift_up(xin, off)
    xr = xr + lw_ref[j:j + 1, :] * win[:ts]
  ri = _dot(xr.astype(BF16), wri_ref[...]) + bri_ref[...]
  r = _sigmoid(ri[:, :LRU_WIDTH])
  i = _sigmoid(ri[:, LRU_WIDTH:])
  log_a = (-LRU_C * r) * _softplus(-lam_ref[...])
  a = jnp.exp(log_a)
  th = jnp.tanh(log_a)
  u = jnp.sqrt(-2.0 * th / (1.0 - th)) * (i * xr)
  h, hcar[...] = _linear_scan(a, u, hcar[...])
  br_c = _dot((_gelu_tanh(y_ref[...]) * h).astype(BF16), pc_ref[...])

  o_ref[...] = ga_ref[...] * br_a + gc_ref[...] * br_c


def _local_call(l, vg, y, xin, gates, w):
  bsz, seq, _ = vg.shape
  d = D_MODEL
  ts = TS_LOCAL
  row = lambda width, col=0: pl.BlockSpec((None, ts, width), lambda b, t: (b, t, col))
  weights = [w["cw"], w["cb"], w["lng"], w["lnb"], w["pa"],
             w["lw"], w["lb"], w["wri"], w["bri"], w["lam"], w["pc"]]
  return pl.pallas_call(
      _local_kernel,
      out_shape=jax.ShapeDtypeStruct((bsz, seq, d), F32),
      grid=(bsz, seq // ts),
      in_specs=[row(CONV_WIDTH), row(LRU_WIDTH), row(LRU_WIDTH), row(d, 0), row(d, 2)]
      + [_layer_block(a, l) for a in weights],
      out_specs=row(d),
      scratch_shapes=[
          pltpu.VMEM((CONV_HALO + ts, CONV_WIDTH), F32),
          pltpu.VMEM((SUBLANES - 1, CONV_HALO + ts - SUBLANES, CONV_WIDTH), F32),
          pltpu.VMEM((ts, CONV_WIDTH), F32),
          pltpu.VMEM((LRU_HALO + ts, LRU_WIDTH), F32),
          pltpu.VMEM((1, LRU_WIDTH), F32),
      ],
      compiler_params=_params("arbitrary", "arbitrary"),
      name="local",
  )(vg, y, xin, gates, gates, *weights)


def _attn_kernel(q_ref, k_ref, qa_ref, ka_ref, v_ref, o_ref,
                 qh_ref, kk_ref, vaug_ref, m_ref, acc_ref):
  seq = q_ref.shape[0]
  lane = lax.broadcasted_iota(jnp.int32, (seq, LANES), 1)
  q2 = q_ref[...].astype(F32)
  qa2 = qa_ref[...].astype(F32)
  v2 = v_ref[...].astype(F32)
  for hh in range(2):
    own_dims = (lane < FOX_HEAD_DIM) if hh == 0 else (lane >= FOX_HEAD_DIM)
    own_bias = (lane < AUG_COLS) if hh == 0 else (lane >= AUG_COLS)
    qh_ref[hh, :, :LANES] = jnp.where(own_dims, q2, 0.0).astype(BF16)
    qh_ref[hh, :, LANES:] = jnp.where(own_bias, qa2, 0.0).astype(BF16)
    ones_lane = FOX_HEAD_DIM if hh == 0 else 0
    vaug_ref[hh] = jnp.where(own_dims, v2, jnp.where(lane == ones_lane, 1.0, 0.0)).astype(BF16)
  kk_ref[:, :LANES] = k_ref[...]
  kk_ref[:, LANES:] = ka_ref[...]
  tq = m_ref.shape[1]

  def update(hh, q0, k0, r0, diagonal):
    rows = tq - r0
    q = qh_ref[hh, pl.ds(q0 + r0, rows), :]
    k = kk_ref[pl.ds(k0, TK), :]
    s = lax.dot_general(q, k, (((1,), (1,)), ((), ())), preferred_element_type=F32)
    if diagonal:
      row = lax.broadcasted_iota(jnp.int32, (TK, TK), 0)
      col = lax.broadcasted_iota(jnp.int32, (TK, TK), 1)
      top = jnp.where(row >= col, s[:TK], NEG_BIG)
      s = top if rows == TK else jnp.concatenate([top, s[TK:]], axis=0)
    m_prev = m_ref[hh, r0:, :]
    m_next = jnp.maximum(m_prev, jnp.max(s, axis=1, keepdims=True))
    p = jnp.exp2(s - jnp.concatenate([m_next] * (TK // LANES), axis=1))
    alpha = jnp.exp2(m_prev - m_next)
    pv = _dot(p.astype(BF16), vaug_ref[hh, pl.ds(k0, TK), :])
    acc_ref[hh, r0:, :] = alpha * acc_ref[hh, r0:, :] + pv
    m_ref[hh, r0:, :] = m_next

  def q_tile(i, carry):
    q0 = pl.multiple_of(i * tq, tq)
    m_ref[...] = jnp.full(m_ref.shape, NEG_BIG, F32)
    acc_ref[...] = jnp.zeros_like(acc_ref)

    def kv_step(j, c):
      k0 = pl.multiple_of(j * TK, TK)
      for hh in range(2):
        update(hh, q0, k0, 0, False)
      return c

    lax.fori_loop(0, i * (tq // TK), kv_step, 0)
    for r0 in range(0, tq, TK):
      for hh in range(2):
        update(hh, q0, pl.multiple_of(q0 + r0, TK), r0, True)
    a0 = acc_ref[0]
    a1 = acc_ref[1]
    lane = lax.broadcasted_iota(jnp.int32, a0.shape, 1)
    o0 = a0 / a0[:, FOX_HEAD_DIM:FOX_HEAD_DIM + 1]
    o1 = a1 / a1[:, 0:1]
    o_ref[pl.ds(q0, tq), :] = jnp.where(lane < FOX_HEAD_DIM, o0, o1).astype(BF16)
    return carry

  lax.fori_loop(0, seq // tq, q_tile, 0)


def _attn_call(q, k, qa, ka, v):
  bsz, seq, _ = q.shape
  tq = min(TQ, seq)
  assert tq % TK == 0 and seq % tq == 0
  pair = pl.BlockSpec((None, seq, LANES), lambda b, h: (b, 0, h))
  return pl.pallas_call(
      _attn_kernel,
      out_shape=jax.ShapeDtypeStruct((bsz, seq, FOX_WIDTH), BF16),
      grid=(bsz, HEAD_PAIRS),
      in_specs=[pair] * 5,
      out_specs=pair,
      scratch_shapes=[
          pltpu.VMEM((2, seq, 2 * LANES), BF16),
          pltpu.VMEM((seq, 2 * LANES), BF16),
          pltpu.VMEM((2, seq, LANES), BF16),
          pltpu.VMEM((2, tq, LANES), F32),
          pltpu.VMEM((2, tq, LANES), F32),
      ],
      compiler_params=_params("arbitrary", "arbitrary"),
      name="attn",
  )(q, k, qa, ka, v)


def _mlp_kernel(x_ref, o_ref, gb_ref, mac_ref, gt1_ref, sc_ref, sh_ref, gt2_ref, g_ref,
                pb_ref, wo_ref, w1_ref, w2_ref, out_ref):
  br_b = _dot(o_ref[...], pb_ref[...])
  merged = mac_ref[...] + gb_ref[...] * br_b
  x1 = x_ref[...] + gt1_ref[...] * _dot(merged.astype(BF16), wo_ref[...])
  hb = _modulated_norm(x1, g_ref[...], sc_ref[...], sh_ref[...]).astype(BF16)
  ffn = None
  for c0, cn in FFN_CHUNKS:
    a = _dot(hb, w1_ref[:, c0:c0 + cn])
    b = _dot(hb, w1_ref[:, FFN_HIDDEN + c0:FFN_HIDDEN + c0 + cn])
    part = _dot(((a * _sigmoid(a)) * b).astype(BF16), w2_ref[c0:c0 + cn, :])
    ffn = part if ffn is None else ffn + part
  out_ref[...] = x1 + gt2_ref[...] * ffn


def _mlp_call(l, x, o, gates, mac, gt1, sc2, sh2, gt2, w):
  bsz, seq, d = x.shape
  ts = TS_MLP
  row = lambda width, col=0: pl.BlockSpec((None, ts, width), lambda b, t: (b, t, col))
  per_batch = pl.BlockSpec((None, 1, d), lambda b, t: (b, 0, 0))
  weights = [w["g_ffn"], w["pb"], w["wo"], w["w1"], w["w2"]]
  return pl.pallas_call(
      _mlp_kernel,
      out_shape=jax.ShapeDtypeStruct((bsz, seq, d), F32),
      grid=(bsz, seq // ts),
      in_specs=[row(d), row(FOX_WIDTH), row(d, 1), row(d),
                per_batch, per_batch, per_batch, per_batch]
      + [_layer_block(a, l) for a in weights],
      out_specs=row(d),
      compiler_params=_params("arbitrary", "arbitrary"),
      name="mlp",
  )(x, o, gates, mac, gt1, sc2, sh2, gt2, *weights)


def _block_diag(w):
  depth, n, a, b = w.shape
  eye = jnp.eye(n, dtype=w.dtype)
  return (eye[None, :, None, :, None] * w[:, :, :, None, :]).reshape(depth, n * a, n * b)


def _placement_consts(ts):
  plq = np.zeros((LANES, HEAD_PAIRS * LANES), np.float32)
  plk = np.zeros((LANES, HEAD_PAIRS * LANES), np.float32)
  one_lane = FOX_HEADS * CUM_PARTS
  for hh in range(FOX_HEADS):
    base = (hh // 2) * LANES + (hh % 2) * AUG_COLS
    for i in range(CUM_PARTS):
      plq[FOX_HEADS * i + hh, base + i] = 1.0
      plq[one_lane, base + CUM_PARTS + i] = 1.0
      plk[one_lane, base + i] = 1.0
      plk[FOX_HEADS * i + hh, base + CUM_PARTS + i] = -1.0
  tri = np.tril(np.ones((ts, ts), np.float32))
  return {"plq": jnp.asarray(plq, BF16), "plk": jnp.asarray(plk, BF16),
          "tri": jnp.asarray(tri, BF16)}


def _stacked_weights(p):
  w_in = p["w_in"]
  o_glu = 0
  o_q = o_glu + 2 * CONV_WIDTH
  o_k = o_q + FOX_WIDTH
  o_v = o_k + FOX_WIDTH
  o_f = o_v + FOX_WIDTH
  o_y = o_f + FOX_HEADS
  o_g = o_y + 2 * LRU_WIDTH
  assert (o_q, o_k, o_v) == (2 * FOX_WIDTH, 3 * FOX_WIDTH, 4 * FOX_WIDTH)
  pad_f =LANES - CUM_PARTS * FOX_HEADS
  wf = jnp.pad(jnp.tile(w_in[:, :, o_f:o_y], (1, 1, CUM_PARTS)), ((0, 0), (0, 0), (0, pad_f)))
  bf = jnp.pad(jnp.tile(p["fox_b_f"], (1, CUM_PARTS)), ((0, 0), (0, pad_f)))
  scale = FOX_HEAD_DIM ** -0.5 * LOG2E
  gq = jnp.tile(p["fox_q_norm_g"] * scale, (1, FOX_HEADS))
  gk = jnp.tile(p["fox_k_norm_g"], (1, FOX_HEADS))
  row = lambda v: v[:, None, :]
  w_in_b = w_in.astype(BF16)
  return {
      "g_mix": row(p["g_norm_mix"]), "g_ffn": row(p["g_norm_ffn"]),
      "in_head": w_in_b,
      "f": wf.astype(BF16),
      "yx": w_in_b[:, :, o_y:o_g],
      "gate": w_in_b[:, :, o_g:],
      "bf": row(bf), "gq": row(gq), "gk": row(gk),
      "cw": p["conv_dw_w"], "cb": row(p["conv_dw_b"]),
      "lng": row(p["conv_ln_g"]), "lnb": row(p["conv_ln_b"]),
      "pa": p["w_conv_out"].astype(BF16),
      "lw": p["lru_conv_w"], "lb": row(p["lru_conv_b"]),
      "wri": jnp.concatenate([_block_diag(p["lru_w_r"]), _block_diag(p["lru_w_i"])],
                             axis=2).astype(BF16),
      "bri": row(jnp.concatenate([p["lru_b_r"], p["lru_b_i"]], axis=1)),
      "lam": row(p["lru_lambda"]),
      "pc": p["w_lru_out"].astype(BF16),
      "pb": p["w_fox_out"].astype(BF16),
      "wo": p["w_o"].astype(BF16),
      "w1": p["w_ffn_in"].astype(BF16),
      "w2": p["w_ffn_out"].astype(BF16),
  }


def kernel(x, c, w_ada, b_ada, g_norm_mix, g_norm_ffn, w_in, conv_dw_w, conv_dw_b, conv_ln_g, conv_ln_b, w_conv_out, fox_b_f, fox_q_norm_g, fox_k_norm_g, w_fox_out, lru_conv_w, lru_conv_b, lru_w_r, lru_b_r, lru_w_i, lru_b_i, lru_lambda, w_lru_out, w_o, w_ffn_in, w_ffn_out):
  p = dict(g_norm_mix=g_norm_mix, g_norm_ffn=g_norm_ffn, w_in=w_in,
           conv_dw_w=conv_dw_w, conv_dw_b=conv_dw_b, conv_ln_g=conv_ln_g,
           conv_ln_b=conv_ln_b, w_conv_out=w_conv_out, fox_b_f=fox_b_f,
           fox_q_norm_g=fox_q_norm_g, fox_k_norm_g=fox_k_norm_g, w_fox_out=w_fox_out,
           lru_conv_w=lru_conv_w, lru_conv_b=lru_conv_b, lru_w_r=lru_w_r, lru_b_r=lru_b_r,
           lru_w_i=lru_w_i, lru_b_i=lru_b_i, lru_lambda=lru_lambda, w_lru_out=w_lru_out,
           w_o=w_o, w_ffn_in=w_ffn_in, w_ffn_out=w_ffn_out)
  bsz, seq, d = x.shape
  depth = w_ada.shape[0]
  assert d == D_MODEL and seq % max(TS_PROJ, TS_LOCAL, TS_MLP) == 0
  consts = _placement_consts(TS_PROJ)
  w = _stacked_weights(p)
  mod = _ada_call(c, w_ada, b_ada)
  for l in range(depth):
    sh1, sc1, gt1, sh2, sc2, gt2 = [m[:, None, :] for m in jnp.split(mod[l], 6, axis=-1)]
    vg, q, k, qa, ka, v, y, xin, gates = _inproj_call(l, x, sc1, sh1, w, consts)
    mac = _local_call(l, vg, y, xin, gates, w)
    o = _attn_call(q, k, qa, ka, v)
    x = _mlp_call(l, x, o, gates, mac, gt1, sc2, sh2, gt2, w)
  return x
```

```python
import jax
import jax.numpy as jnp
import numpy as np
from jax import lax
from jax.experimental import pallas as pl
from jax.experimental.pallas import tpu as pltpu

F32 = jnp.float32
BF16 = jnp.bfloat16

D_MODEL = 1024
EPS = 1e-6
CONV_WIDTH = 512
CONV_KERNEL = 31
FOX_HEADS = 8
FOX_HEAD_DIM = 64
FOX_WIDTH = FOX_HEADS * FOX_HEAD_DIM
LRU_WIDTH = 512
LRU_BLOCKS = 8
LRU_CONV = 4
LRU_C = 8.0
N_BRANCH = 3
FFN_HIDDEN = 2816

LANES = 128
SUBLANES = 8
HEAD_PAIRS = FOX_HEADS // 2
CUM_PARTS = 3
AUG_COLS = 2 * CUM_PARTS
CONV_HALO = 32
LRU_HALO = SUBLANES
VMEM_LIMIT = 56 * 1024 * 1024
LOG2E = float(np.log2(np.e))
NEG_BIG = -1e30

TS_PROJ = 256
TS_LOCAL = 512
TQ = 2048
TK = 512
TS_MLP = 512
CONV_ROWS = 64
FFN_CHUNKS = ((0, 1024), (1024, 1024), (2048, 768))


def _dot(a, b):
  return jnp.dot(a, b, preferred_element_type=F32)


def _sigmoid(x):
  return 1.0 / (1.0 + jnp.exp(-x))


def _softplus(x):
  return jnp.maximum(x, 0.0) + jnp.log1p(jnp.exp(-jnp.abs(x)))


def _split_bf16(x, parts):
  out = []
  rem = x
  for _ in range(parts):
    p = rem.astype(BF16)
    out.append(p)
    rem = rem - p.astype(F32)
  return out


def _resident(shape):
  nd = len(shape)
  return pl.BlockSpec(shape, lambda *_: (0,) * nd, pipeline_mode=pl.Buffered(1))


def _layer_block(stacked, l):
  shape = stacked.shape[1:]
  nd = len(shape)
  return pl.BlockSpec((None,) + shape, lambda *_: (l,) + (0,) * nd,
                      pipeline_mode=pl.Buffered(1))


def _layer_cols(stacked, l, width, idx):
  rows = stacked.shape[1]
  return pl.BlockSpec((None, rows, width), lambda *_: (l, 0, idx),
                      pipeline_mode=pl.Buffered(1))


def _params(*sem):
  return pltpu.CompilerParams(dimension_semantics=sem, vmem_limit_bytes=VMEM_LIMIT)


def _ada_kernel(c_ref, w_ref, b_ref, o_ref):
  c = c_ref[...]
  act = (c * _sigmoid(c)).astype(BF16)
  o_ref[...] = _dot(act, w_ref[...].astype(BF16)) + b_ref[...]


def _ada_call(c, w_ada, b_ada):
  depth, d, n = w_ada.shape
  bsz = c.shape[0]
  tn = d
  return pl.pallas_call(
      _ada_kernel,
      out_shape=jax.ShapeDtypeStruct((depth, bsz, n), F32),
      grid=(depth, n // tn),
      in_specs=[
          pl.BlockSpec((bsz, d), lambda l, j: (0, 0)),
          pl.BlockSpec((None, d, tn), lambda l, j: (l, 0, j)),
          pl.BlockSpec((None, 1, tn), lambda l, j: (l, 0, j)),
      ],
      out_specs=pl.BlockSpec((None, bsz, tn), lambda l, j: (l, 0, j)),
      compiler_params=_params("arbitrary", "arbitrary"),
      name="ada",
  )(c, w_ada, b_ada.reshape(depth, 1, n))


def _gelu_tanh(x):
  c = np.float32(np.sqrt(2.0 / np.pi))
  return 0.5 * x * (1.0 + jnp.tanh(c * (x + 0.044715 * (x * x * x))))


def _shift_up(x, r):
  rows, width = x.shape
  x3 = x.reshape(rows // SUBLANES, SUBLANES, width)
  rot = pltpu.roll(x3, SUBLANES - r, 1)
  sub = lax.broadcasted_iota(jnp.int32, (rows // SUBLANES - 1, SUBLANES, width), 1)
  out = jnp.where(sub < SUBLANES - r, rot[:-1], rot[1:])
  return out.reshape(rows - SUBLANES, width)


def _modulated_norm(x, g, sc, sh):
  ms = jnp.mean(x * x, axis=-1, keepdims=True)
  h = (x * lax.rsqrt(ms + EPS)) * g
  return h * (1.0 + sc) + sh


def _head_norm(z):
  parts = []
  for hp in range(HEAD_PAIRS):
    s = z[:, hp * LANES:(hp + 1) * LANES]
    low = lax.broadcasted_iota(jnp.int32, s.shape, 1) < FOX_HEAD_DIM
    sq = s * s
    ssq = jnp.where(low,
                    jnp.sum(jnp.where(low, sq, 0.0), axis=-1, keepdims=True),
                    jnp.sum(jnp.where(low, 0.0, sq), axis=-1, keepdims=True))
    parts.append(s * lax.rsqrt(ssq * (1.0 / FOX_HEAD_DIM) + EPS))
  return jnp.concatenate(parts, axis=-1)


def _inproj_kernel(x_ref, sc_ref, sh_ref, g_ref,
                   wglu_ref, wq_ref, wk_ref, wv_ref, wf_ref, wyx_ref, wg_ref,
                   bf_ref, gq_ref, gk_ref, lw_ref, lb_ref, tri_ref, plq_ref, plk_ref,
                   vg_ref, q_ref, k_ref, qa_ref, ka_ref, v_ref, gy_ref, xr_ref, gate_ref,
                   carry_ref, lbuf):
  ts = x_ref.shape[0]
  t = pl.program_id(1)

  @pl.when(t == 0)
  def _():
    carry_ref[...] = jnp.zeros_like(carry_ref)
    lbuf[0:LRU_HALO, :] = jnp.zeros((LRU_HALO, LRU_WIDTH), F32)

  @pl.when(t > 0)
  def _():
    lbuf[0:LRU_HALO, :] = lbuf[ts:ts + LRU_HALO, :]

  hb = _modulated_norm(x_ref[...], g_ref[...], sc_ref[...], sh_ref[...]).astype(BF16)

  zg = _dot(hb, wglu_ref[...])
  vg_ref[...] = zg[:, :CONV_WIDTH] * _sigmoid(zg[:, CONV_WIDTH:])

  zf = _dot(hb, wf_ref[...]) + bf_ref[...]
  lf = jnp.minimum(zf, 0.0) - jnp.log1p(jnp.exp(-jnp.abs(zf)))

  tri = tri_ref[...]
  cum = carry_ref[...]
  for piece in _split_bf16(lf, CUM_PARTS):
    cum = cum + _dot(tri, piece)
  carry_ref[...] = cum[ts - 1:ts, :]

  lane = lax.broadcasted_iota(jnp.int32, (ts, LANES), 1)
  pieces = _split_bf16(cum * LOG2E, CUM_PARTS)
  src = jnp.where(lane == FOX_HEADS * CUM_PARTS, 1.0, 0.0)
  for i in reversed(range(CUM_PARTS)):
    src = jnp.where(lane < FOX_HEADS * (i + 1), pieces[i].astype(F32), src)
  src = src.astype(BF16)

  q_ref[...] = (_head_norm(_dot(hb, wq_ref[...])) * gq_ref[...]).astype(BF16)
  k_ref[...] = (_head_norm(_dot(hb, wk_ref[...])) * gk_ref[...]).astype(BF16)
  qa_ref[...] = _dot(src, plq_ref[...]).astype(BF16)
  ka_ref[...] = _dot(src, plk_ref[...]).astype(BF16)
  v_ref[...] = _dot(hb, wv_ref[...]).astype(BF16)

  zyx = _dot(hb, wyx_ref[...])
  gy_ref[...] = _gelu_tanh(zyx[:, :LRU_WIDTH])
  lbuf[LRU_HALO:LRU_HALO + ts, :] = zyx[:, LRU_WIDTH:]
  first = LRU_HALO - (LRU_CONV - 1)
  xin = lbuf[...]
  xr = jnp.broadcast_to(lb_ref[...], (ts, LRU_WIDTH))
  for j in range(LRU_CONV):
    off = first + j
    win = xin[off:, :] if off % SUBLANES == 0 else _shift_up(xin, off)
    xr = xr + lw_ref[j:j + 1, :] * win[:ts]
  xr_ref[...] = xr

  gate_ref[...] = _sigmoid(_dot(hb, wg_ref[...]))


def _inproj_call(l, x, sc, sh, w, consts):
  bsz, seq, d = x.shape
  ts = TS_PROJ
  grid = (bsz, seq // ts)
  row = lambda width: pl.BlockSpec((None, ts, width), lambda b, t: (b, t, 0))
  per_batch = pl.BlockSpec((None, 1, d), lambda b, t: (b, 0, 0))
  head = w["in_head"]
  head_cols = [(2 * CONV_WIDTH, 0), (FOX_WIDTH, 2), (FOX_WIDTH, 3), (FOX_WIDTH, 4)]
  weights = [w["f"], w["yx"], w["gate"], w["bf"], w["gq"], w["gk"], w["lw"], w["lb"]]
  shared = [consts["tri"], consts["plq"], consts["plk"]]
  out_widths = [(CONV_WIDTH, F32), (FOX_WIDTH, BF16), (FOX_WIDTH, BF16),
                (HEAD_PAIRS * LANES, BF16), (HEAD_PAIRS * LANES, BF16), (FOX_WIDTH, BF16),
                (LRU_WIDTH, F32), (LRU_WIDTH, F32), (N_BRANCH * d, F32)]
  return pl.pallas_call(
      _inproj_kernel,
      out_shape=[jax.ShapeDtypeStruct((bsz, seq, wd), dt) for wd, dt in out_widths],
      grid=grid,
      in_specs=[row(d), per_batch, per_batch, _layer_block(w["g_mix"], l)]
      + [_layer_cols(head, l, width, idx) for width, idx in head_cols]
      + [_layer_block(a, l) for a in weights] + [_resident(a.shape) for a in shared],
      out_specs=[row(wd) for wd, _ in out_widths],
      scratch_shapes=[pltpu.VMEM((1, LANES), F32),
                      pltpu.VMEM((LRU_HALO + ts, LRU_WIDTH), F32)],
      compiler_params=_params("arbitrary", "arbitrary"),
      name="in_proj",
  )(x, sc, sh, w["g_mix"], head, head, head, head, *weights, *shared)


def _linear_scan(a, u, carry):
  ts, width = a.shape
  a = a.reshape(ts // SUBLANES, SUBLANES, width)
  u = u.reshape(ts // SUBLANES, SUBLANES, width)
  sub = lax.broadcasted_iota(jnp.int32, a.shape, 1)
  d = 1
  while d < SUBLANES:
    keep = sub >= d
    u = u + a * jnp.where(keep, pltpu.roll(u, d, 1), 0.0)
    a = a * jnp.where(keep, pltpu.roll(a, d, 1), 1.0)
    d *= 2
  groups = []
  for g in range(ts // SUBLANES):
    h = u[g] + a[g] * carry
    groups.append(h)
    carry = h[SUBLANES - 1:SUBLANES]
  return jnp.concatenate(groups, axis=0), carry


def _local_kernel(vg_ref, gy_ref, xr_ref, ga_ref, gc_ref,
                  cw_ref, cb_ref, lng_ref, lnb_ref, pa_ref,
                  wri_ref, bri_ref, lam_ref, pc_ref,
                  o_ref, cbuf, csh, cvo, hcar):
  ts = vg_ref.shape[0]
  t = pl.program_id(1)

  @pl.when(t == 0)
  def _():
    cbuf[0:CONV_HALO, :] = jnp.zeros((CONV_HALO, CONV_WIDTH), F32)
    hcar[...] = jnp.zeros_like(hcar)

  @pl.when(t > 0)
  def _():
    cbuf[0:CONV_HALO, :] = cbuf[ts:ts + CONV_HALO, :]

  cbuf[CONV_HALO:CONV_HALO + ts, :] = vg_ref[...]

  first = CONV_HALO - (CONV_KERNEL - 1)
  for r in range(1, SUBLANES):
    csh[r - 1] = _shift_up(cbuf[...], r)
  for r0 in range(0, ts, CONV_ROWS):
    acc = jnp.broadcast_to(cb_ref[...], (CONV_ROWS, CONV_WIDTH))
    for j in range(CONV_KERNEL):
      r = (first + j) % SUBLANES
      lo = r0 + (first + j) - r
      win = cbuf[lo:lo + CONV_ROWS, :] if r == 0 else csh[r - 1, lo:lo + CONV_ROWS, :]
      acc = acc + cw_ref[j:j + 1, :] * win
    cvo[r0:r0 + CONV_ROWS, :] = acc

  cv = cvo[...]
  mu = jnp.mean(cv, axis=-1, keepdims=True)
  cen = cv - mu
  var = jnp.mean(cen * cen, axis=-1, keepdims=True)
  ln = (cen * lax.rsqrt(var + EPS)) * lng_ref[...] + lnb_ref[...]
  br_a = _dot((ln * _sigmoid(ln)).astype(BF16), pa_ref[...])

  xr = xr_ref[...]
  ri = _dot(xr.astype(BF16), wri_ref[...]) + bri_ref[...]
  r = _sigmoid(ri[:, :LRU_WIDTH])
  i = _sigmoid(ri[:, LRU_WIDTH:])
  log_a = (-LRU_C * r) * _softplus(-lam_ref[...])
  a = jnp.exp(log_a)
  th = jnp.tanh(log_a)
  u = jnp.sqrt(-2.0 * th / (1.0 - th)) * (i * xr)
  h, hcar[...] = _linear_scan(a, u, hcar[...])
  br_c = _dot((gy_ref[...] * h).astype(BF16), pc_ref[...])

  o_ref[...] = ga_ref[...] * br_a + gc_ref[...] * br_c


def _local_call(l, vg, gy, xr, gates, w):
  bsz, seq, _ = vg.shape
  d = D_MODEL
  ts = TS_LOCAL
  row = lambda width, col=0: pl.BlockSpec((None, ts, width), lambda b, t: (b, t, col))
  weights = [w["cw"], w["cb"], w["lng"], w["lnb"], w["pa"],
             w["wri"], w["bri"], w["lam"], w["pc"]]
  return pl.pallas_call(
      _local_kernel,
      out_shape=jax.ShapeDtypeStruct((bsz, seq, d), F32),
      grid=(bsz, seq // ts),
      in_specs=[row(CONV_WIDTH), row(LRU_WIDTH), row(LRU_WIDTH), row(d, 0), row(d, 2)]
      + [_layer_block(a, l) for a in weights],
      out_specs=row(d),
      scratch_shapes=[
          pltpu.VMEM((CONV_HALO + ts, CONV_WIDTH), F32),
          pltpu.VMEM((SUBLANES - 1, CONV_HALO + ts - SUBLANES, CONV_WIDTH), F32),
          pltpu.VMEM((ts, CONV_WIDTH), F32),
          pltpu.VMEM((1, LRU_WIDTH), F32),
      ],
      compiler_params=_params("arbitrary", "arbitrary"),
      name="local",
  )(vg, gy, xr, gates, gates, *weights)


def _attn_kernel(q_ref, k_ref, qa_ref, ka_ref, v_ref, o_ref,
                 qh_ref, kk_ref, vaug_ref, m_ref, acc_ref):
  seq = q_ref.shape[0]
  lane = lax.broadcasted_iota(jnp.int32, (seq, LANES), 1)
  q2 = q_ref[...].astype(F32)
  qa2 = qa_ref[...].astype(F32)
  v2 = v_ref[...].astype(F32)
  for hh in range(2):
    own_dims = (lane < FOX_HEAD_DIM) if hh == 0 else (lane >= FOX_HEAD_DIM)
    own_bias = (lane < AUG_COLS) if hh == 0 else (lane >= AUG_COLS)
    qh_ref[hh, :, :LANES] = jnp.where(own_dims, q2, 0.0).astype(BF16)
    qh_ref[hh, :, LANES:] = jnp.where(own_bias, qa2, 0.0).astype(BF16)
    ones_lane = FOX_HEAD_DIM if hh == 0 else 0
    vaug_ref[hh] = jnp.where(own_dims, v2, jnp.where(lane == ones_lane, 1.0, 0.0)).astype(BF16)
  kk_ref[:, :LANES] = k_ref[...]
  kk_ref[:, LANES:] = ka_ref[...]
  tq = m_ref.shape[1]

  def update(hh, q0, k0, r0, diagonal):
    rows = tq - r0
    q = qh_ref[hh, pl.ds(q0 + r0, rows), :]
    k = kk_ref[pl.ds(k0, TK), :]
    s = lax.dot_general(q, k, (((1,), (1,)), ((), ())), preferred_element_type=F32)
    if diagonal:
      row = lax.broadcasted_iota(jnp.int32, (TK, TK), 0)
      col = lax.broadcasted_iota(jnp.int32, (TK, TK), 1)
      top = jnp.where(row >= col, s[:TK], NEG_BIG)
      s = top if rows == TK else jnp.concatenate([top, s[TK:]], axis=0)
    m_prev = m_ref[hh, r0:, :]
    m_next = jnp.maximum(m_prev, jnp.max(s, axis=1, keepdims=True))
    p = jnp.exp2(s - jnp.concatenate([m_next] * (TK // LANES), axis=1))
    alpha = jnp.exp2(m_prev - m_next)
    pv = _dot(p.astype(BF16), vaug_ref[hh, pl.ds(k0, TK), :])
    acc_ref[hh, r0:, :] = alpha * acc_ref[hh, r0:, :] + pv
    m_ref[hh, r0:, :] = m_next

  def q_tile(i, carry):
    q0 = pl.multiple_of(i * tq, tq)
    m_ref[...] = jnp.full(m_ref.shape, NEG_BIG, F32)
    acc_ref[...] = jnp.zeros_like(acc_ref)

    def kv_step(j, c):
      k0 = pl.multiple_of(j * TK, TK)
      for hh in range(2):
        update(hh, q0, k0, 0, False)
      return c

    lax.fori_loop(0, i * (tq // TK), kv_step, 0)
    for r0 in range(0, tq, TK):
      for hh in range(2):
        update(hh, q0, pl.multiple_of(q0 + r0, TK), r0, True)
    a0 = acc_ref[0]
    a1 = acc_ref[1]
    lane = lax.broadcasted_iota(jnp.int32, a0.shape, 1)
    o0 = a0 / a0[:, FOX_HEAD_DIM:FOX_HEAD_DIM + 1]
    o1 = a1 / a1[:, 0:1]
    o_ref[pl.ds(q0, tq), :] = jnp.where(lane < FOX_HEAD_DIM, o0, o1).astype(BF16)
    return carry

  lax.fori_loop(0, seq // tq, q_tile, 0)


def _attn_call(q, k, qa, ka, v):
  bsz, seq, _ = q.shape
  tq = min(TQ, seq)
  assert tq % TK == 0 and seq % tq == 0
  pair = pl.BlockSpec((None, seq, LANES), lambda b, h: (b, 0, h))
  return pl.pallas_call(
      _attn_kernel,
      out_shape=jax.ShapeDtypeStruct((bsz, seq, FOX_WIDTH), BF16),
      grid=(bsz, HEAD_PAIRS),
      in_specs=[pair] * 5,
      out_specs=pair,
      scratch_shapes=[
          pltpu.VMEM((2, seq, 2 * LANES), BF16),
          pltpu.VMEM((seq, 2 * LANES), BF16),
          pltpu.VMEM((2, seq, LANES), BF16),
          pltpu.VMEM((2, tq, LANES), F32),
          pltpu.VMEM((2, tq, LANES), F32),
      ],
      compiler_params=_params("arbitrary", "arbitrary"),
      name="attn",
  )(q, k, qa, ka, v)


def _mlp_kernel(x_ref, o_ref, gb_ref, mac_ref, gt1_ref, sc_ref, sh_ref, gt2_ref, g_ref,
                pb_ref, wo_ref, w1_ref, w2_ref, out_ref):
  br_b = _dot(o_ref[...], pb_ref[...])
  merged = mac_ref[...] + gb_ref[...] * br_b
  x1 = x_ref[...] + gt1_ref[...] * _dot(merged.astype(BF16), wo_ref[...])
  hb = _modulated_norm(x1, g_ref[...], sc_ref[...], sh_ref[...]).astype(BF16)
  ffn = None
  for c0, cn in FFN_CHUNKS:
    a = _dot(hb, w1_ref[:, c0:c0 + cn])
    b = _dot(hb, w1_ref[:, FFN_HIDDEN + c0:FFN_HIDDEN + c0 + cn])
    part = _dot(((a * _sigmoid(a)) * b).astype(BF16), w2_ref[c0:c0 + cn, :])
    ffn = part if ffn is None else ffn + part
  out_ref[...] = x1 + gt2_ref[...] * ffn


def _mlp_call(l, x, o, gates, mac, gt1, sc2, sh2, gt2, w):
  bsz, seq, d = x.shape
  ts = TS_MLP
  row = lambda width, col=0: pl.BlockSpec((None, ts, width), lambda b, t: (b, t, col))
  per_batch = pl.BlockSpec((None, 1, d), lambda b, t: (b, 0, 0))
  weights = [w["g_ffn"], w["pb"], w["wo"], w["w1"], w["w2"]]
  return pl.pallas_call(
      _mlp_kernel,
      out_shape=jax.ShapeDtypeStruct((bsz, seq, d), F32),
      grid=(bsz, seq // ts),
      in_specs=[row(d), row(FOX_WIDTH), row(d, 1), row(d),
                per_batch, per_batch, per_batch, per_batch]
      + [_layer_block(a, l) for a in weights],
      out_specs=row(d),
      compiler_params=_params("arbitrary", "arbitrary"),
      name="mlp",
  )(x, o, gates, mac, gt1, sc2, sh2, gt2, *weights)


def _block_diag(w):
  depth, n, a, b = w.shape
  eye = jnp.eye(n, dtype=w.dtype)
  return (eye[None, :, None, :, None] * w[:, :, :, None, :]).reshape(depth, n * a, n * b)


def _placement_consts(ts):
  plq = np.zeros((LANES, HEAD_PAIRS * LANES), np.float32)
  plk = np.zeros((LANES, HEAD_PAIRS * LANES), np.float32)
  one_lane = FOX_HEADS * CUM_PARTS
  for hh in range(FOX_HEADS):
    base = (hh // 2) * LANES + (hh % 2) * AUG_COLS
    for i in range(CUM_PARTS):
      plq[FOX_HEADS * i + hh, base + i] = 1.0
      plq[one_lane, base + CUM_PARTS + i] = 1.0
      plk[one_lane, base + i] = 1.0
      plk[FOX_HEADS * i + hh, base + CUM_PARTS + i] = -1.0
  tri = np.tril(np.ones((ts, ts), np.float32))
  return {"plq": jnp.asarray(plq, BF16), "plk": jnp.asarray(plk, BF16),
          "tri": jnp.asarray(tri, BF16)}


def _stacked_weights(p):
  w_in = p["w_in"]
  o_glu = 0
  o_q = o_glu + 2 * CONV_WIDTH
  o_k = o_q + FOX_WIDTH
  o_v = o_k + FOX_WIDTH
  o_f = o_v + FOX_WIDTH
  o_y = o_f + FOX_HEADS
  o_g = o_y + 2 * LRU_WIDTH
  assert (o_q, o_k, o_v) == (2 * FOX_WIDTH, 3 * FOX_WIDTH, 4 * FOX_WIDTH)
  pad_f = LANES - CUM_PARTS * FOX_HEADS
  wf = jnp.pad(jnp.tile(w_in[:, :, o_f:o_y], (1, 1, CUM_PARTS)), ((0, 0), (0, 0), (0, pad_f)))
  bf = jnp.pad(jnp.tile(p["fox_b_f"], (1, CUM_PARTS)), ((0, 0), (0, pad_f)))
  scale = FOX_HEAD_DIM ** -0.5 * LOG2E
  gq = jnp.tile(p["fox_q_norm_g"] * scale, (1, FOX_HEADS))
  gk = jnp.tile(p["fox_k_norm_g"], (1, FOX_HEADS))
  row = lambda v: v[:, None, :]
  return {
      "g_mix": row(p["g_norm_mix"]), "g_ffn": row(p["g_norm_ffn"]),
      "in_head": w_in[:, :, :o_f].astype(BF16),
      "f": wf.astype(BF16),
      "yx": w_in[:, :, o_y:o_g].astype(BF16),
      "gate": w_in[:, :, o_g:].astype(BF16),
      "bf": row(bf), "gq": row(gq), "gk": row(gk),
      "cw": p["conv_dw_w"], "cb": row(p["conv_dw_b"]),
      "lng": row(p["conv_ln_g"]), "lnb": row(p["conv_ln_b"]),
      "pa": p["w_conv_out"].astype(BF16),
      "lw": p["lru_conv_w"], "lb": row(p["lru_conv_b"]),
      "wri": jnp.concatenate([_block_diag(p["lru_w_r"]), _block_diag(p["lru_w_i"])],
                             axis=2).astype(BF16),
      "bri": row(jnp.concatenate([p["lru_b_r"], p["lru_b_i"]], axis=1)),
      "lam": row(p["lru_lambda"]),
      "pc": p["w_lru_out"].astype(BF16),
      "pb": p["w_fox_out"].astype(BF16),
      "wo": p["w_o"].astype(BF16),
      "w1": p["w_ffn_in"].astype(BF16),
      "w2": p["w_ffn_out"].astype(BF16),
  }


def kernel(x, c, w_ada, b_ada, g_norm_mix, g_norm_ffn, w_in, conv_dw_w, conv_dw_b, conv_ln_g, conv_ln_b, w_conv_out, fox_b_f, fox_q_norm_g, fox_k_norm_g, w_fox_out, lru_conv_w, lru_conv_b, lru_w_r, lru_b_r, lru_w_i, lru_b_i, lru_lambda, w_lru_out, w_o, w_ffn_in, w_ffn_out):
  p = dict(g_norm_mix=g_norm_mix, g_norm_ffn=g_norm_ffn, w_in=w_in,
           conv_dw_w=conv_dw_w, conv_dw_b=conv_dw_b, conv_ln_g=conv_ln_g,
           conv_ln_b=conv_ln_b, w_conv_out=w_conv_out, fox_b_f=fox_b_f,
           fox_q_norm_g=fox_q_norm_g, fox_k_norm_g=fox_k_norm_g, w_fox_out=w_fox_out,
           lru_conv_w=lru_conv_w, lru_conv_b=lru_conv_b, lru_w_r=lru_w_r, lru_b_r=lru_b_r,
           lru_w_i=lru_w_i, lru_b_i=lru_b_i, lru_lambda=lru_lambda, w_lru_out=w_lru_out,
           w_o=w_o, w_ffn_in=w_ffn_in, w_ffn_out=w_ffn_out)
  bsz, seq, d = x.shape
  depth = w_ada.shape[0]
  assert d == D_MODEL and seq % max(TS_PROJ, TS_LOCAL, TS_MLP) == 0
  consts = _placement_consts(TS_PROJ)
  w = _stacked_weights(p)
  mod = _ada_call(c, w_ada, b_ada)
  for l in range(depth):
    sh1, sc1, gt1, sh2, sc2, gt2 = [m[:, None, :] for m in jnp.split(mod[l], 6, axis=-1)]
    vg, q, k, qa, ka, v, gy, xr, gates = _inproj_call(l, x, sc1, sh1, w, consts)
    mac = _local_call(l, vg, gy, xr, gates, w)
    o = _attn_call(q, k, qa, ka, v)
    x = _mlp_call(l, x, o, gates, mac, gt1, sc2, sh2, gt2, w)
  return x
```

```python
import jax
import jax.numpy as jnp
import numpy as np
from jax import lax
from jax.experimental import pallas as pl
from jax.experimental.pallas import tpu as pltpu

F32 = jnp.float32
BF16 = jnp.bfloat16

D_MODEL = 1024
EPS = 1e-6
CONV_WIDTH = 512
CONV_KERNEL = 31
FOX_HEADS = 8
FOX_HEAD_DIM = 64
FOX_WIDTH = FOX_HEADS * FOX_HEAD_DIM
LRU_WIDTH = 512
LRU_BLOCKS = 8
LRU_CONV = 4
LRU_C = 8.0
N_BRANCH = 3
FFN_HIDDEN = 2816

LANES = 128
SUBLANES = 8
HEAD_PAIRS = FOX_HEADS // 2
CUM_PARTS = 3
AUG_COLS = 2 * CUM_PARTS
CONV_HALO = 32
LRU_HALO = SUBLANES
VMEM_LIMIT = 56 * 1024 * 1024
LOG2E = float(np.log2(np.e))
NEG_BIG = -1e30

TS_PROJ = 256
TS_LOCAL = 512
TQ = 2048
TK = 512
TS_MLP = 512
CONV_ROWS = 64
FFN_CHUNKS = ((0, 1024), (1024, 1024), (2048, 768))


def _dot(a, b):
  return jnp.dot(a, b, preferred_element_type=F32)


def _sigmoid(x):
  return 1.0 / (1.0 + jnp.exp(-x))


def _softplus(x):
  return jnp.maximum(x, 0.0) + jnp.log1p(jnp.exp(-jnp.abs(x)))


def _split_bf16(x, parts):
  out = []
  rem = x
  for _ in range(parts):
    p = rem.astype(BF16)
    out.append(p)
    rem = rem - p.astype(F32)
  return out


def _resident(shape):
  nd = len(shape)
  return pl.BlockSpec(shape, lambda *_: (0,) * nd, pipeline_mode=pl.Buffered(1))


def _layer_block(stacked, l):
  shape = stacked.shape[1:]
  nd = len(shape)
  return pl.BlockSpec((None,) + shape, lambda *_: (l,) + (0,) * nd,
                      pipeline_mode=pl.Buffered(1))


def _layer_cols(stacked, l, width, idx):
  rows = stacked.shape[1]
  return pl.BlockSpec((None, rows, width), lambda *_: (l, 0, idx),
                      pipeline_mode=pl.Buffered(1))


def _params(*sem):
  return pltpu.CompilerParams(dimension_semantics=sem, vmem_limit_bytes=VMEM_LIMIT)


def _ada_kernel(c_ref, w_ref, b_ref, o_ref):
  c = c_ref[...]
  act = (c * _sigmoid(c)).astype(BF16)
  o_ref[...] = _dot(act, w_ref[...].astype(BF16)) + b_ref[...]


def _ada_call(c, w_ada, b_ada):
  depth, d, n = w_ada.shape
  bsz = c.shape[0]
  tn = d
  return pl.pallas_call(
      _ada_kernel,
      out_shape=jax.ShapeDtypeStruct((depth, bsz, n), F32),
      grid=(depth, n // tn),
      in_specs=[
          pl.BlockSpec((bsz, d), lambda l, j: (0, 0)),
          pl.BlockSpec((None, d, tn), lambda l, j: (l, 0, j)),
          pl.BlockSpec((None, 1, tn), lambda l, j: (l, 0, j)),
      ],
      out_specs=pl.BlockSpec((None, bsz, tn), lambda l, j: (l, 0, j)),
      compiler_params=_params("arbitrary", "arbitrary"),
      name="ada",
  )(c, w_ada, b_ada.reshape(depth, 1, n))


def _gelu_tanh(x):
  c = np.float32(np.sqrt(2.0 / np.pi))
  return 0.5 * x * (1.0 + jnp.tanh(c * (x + 0.044715 * (x * x * x))))


def _shift_up(x, r):
  rows, width = x.shape
  x3 = x.reshape(rows // SUBLANES, SUBLANES, width)
  rot = pltpu.roll(x3, SUBLANES - r, 1)
  sub = lax.broadcasted_iota(jnp.int32, (rows // SUBLANES - 1, SUBLANES, width), 1)
  out = jnp.where(sub < SUBLANES - r, rot[:-1], rot[1:])
  return out.reshape(rows - SUBLANES, width)


def _modulated_norm(x, g, sc, sh):
  ms = jnp.mean(x * x, axis=-1, keepdims=True)
  h = (x * lax.rsqrt(ms + EPS)) * g
  return h * (1.0 + sc) + sh


def _head_norm(z):
  parts = []
  for hp in range(HEAD_PAIRS):
    s = z[:, hp * LANES:(hp + 1) * LANES]
    low = lax.broadcasted_iota(jnp.int32, s.shape, 1) < FOX_HEAD_DIM
    sq = s * s
    ssq = jnp.where(low,
                    jnp.sum(jnp.where(low, sq, 0.0), axis=-1, keepdims=True),
                    jnp.sum(jnp.where(low, 0.0, sq), axis=-1, keepdims=True))
    parts.append(s * lax.rsqrt(ssq * (1.0 / FOX_HEAD_DIM) + EPS))
  return jnp.concatenate(parts, axis=-1)


def _inproj_kernel(x_ref, sc_ref, sh_ref, g_ref,
                   wglu_ref, wq_ref, wk_ref, wv_ref, wf_ref, wyx_ref, wg_ref,
                   bf_ref, gq_ref, gk_ref, tri_ref, plq_ref, plk_ref,
                   vg_ref, q_ref, k_ref, qa_ref, ka_ref, v_ref, y_ref, xin_ref, gate_ref,
                   carry_ref):
  ts = x_ref.shape[0]
  t = pl.program_id(1)
  hb = _modulated_norm(x_ref[...], g_ref[...], sc_ref[...], sh_ref[...]).astype(BF16)

  zg = _dot(hb, wglu_ref[...])
  vg_ref[...] = zg[:, :CONV_WIDTH] * _sigmoid(zg[:, CONV_WIDTH:])

  zf = _dot(hb, wf_ref[...]) + bf_ref[...]
  lf = jnp.minimum(zf, 0.0) - jnp.log1p(jnp.exp(-jnp.abs(zf)))

  @pl.when(t == 0)
  def _():
    carry_ref[...] = jnp.zeros_like(carry_ref)

  tri = tri_ref[...]
  cum = carry_ref[...]
  for piece in _split_bf16(lf, CUM_PARTS):
    cum = cum + _dot(tri, piece)
  carry_ref[...] = cum[ts - 1:ts, :]

  lane = lax.broadcasted_iota(jnp.int32, cum.shape, 1)
  pieces = _split_bf16(cum * LOG2E, CUM_PARTS)
  src = jnp.where(lane == FOX_HEADS * CUM_PARTS, 1.0, 0.0)
  for i in reversed(range(CUM_PARTS)):
    src = jnp.where(lane < FOX_HEADS * (i + 1), pieces[i].astype(F32), src)
  src = src.astype(BF16)

  q_ref[...] = (_head_norm(_dot(hb, wq_ref[...])) * gq_ref[...]).astype(BF16)
  k_ref[...] = (_head_norm(_dot(hb, wk_ref[...])) * gk_ref[...]).astype(BF16)
  qa_ref[...] = _dot(src, plq_ref[...]).astype(BF16)
  ka_ref[...] = _dot(src, plk_ref[...]).astype(BF16)

  v_ref[...] = _dot(hb, wv_ref[...]).astype(BF16)

  zyx = _dot(hb, wyx_ref[...])
  y_ref[...] = zyx[:, :LRU_WIDTH]
  xin_ref[...] = zyx[:, LRU_WIDTH:]

  d = x_ref.shape[1]
  zgate = _dot(hb, wg_ref[...])
  gate_ref[:, :d] = _sigmoid(zgate[:, :d])
  gate_ref[:, d:2 * d] = zgate[:, d:2 * d]
  gate_ref[:, 2 * d:] = _sigmoid(zgate[:, 2 * d:])


def _inproj_call(l, x, sc, sh, w, consts):
  bsz, seq, d = x.shape
  ts = TS_PROJ
  grid = (bsz, seq // ts)
  row = lambda width: pl.BlockSpec((None, ts, width), lambda b, t: (b, t, 0))
  per_batch = pl.BlockSpec((None, 1, d), lambda b, t: (b, 0, 0))
  head = w["in_head"]
  head_cols = [(2 * CONV_WIDTH, 0), (FOX_WIDTH, 2), (FOX_WIDTH, 3), (FOX_WIDTH, 4)]
  weights = [w["f"], w["yx"], w["gate"], w["bf"], w["gq"], w["gk"]]
  shared = [consts["tri"], consts["plq"], consts["plk"]]
  out_widths = [(CONV_WIDTH, F32), (FOX_WIDTH, BF16), (FOX_WIDTH, BF16),
                (HEAD_PAIRS * LANES, BF16), (HEAD_PAIRS * LANES, BF16), (FOX_WIDTH, BF16),
                (LRU_WIDTH, F32), (LRU_WIDTH, F32), (N_BRANCH * d, F32)]
  return pl.pallas_call(
      _inproj_kernel,
      out_shape=[jax.ShapeDtypeStruct((bsz, seq, wd), dt) for wd, dt in out_widths],
      grid=grid,
      in_specs=[row(d), per_batch, per_batch, _layer_block(w["g_mix"], l)]
      + [_layer_cols(head, l, width, idx) for width, idx in head_cols]
      + [_layer_block(a, l) for a in weights] + [_resident(a.shape) for a in shared],
      out_specs=[row(wd) for wd, _ in out_widths],
      scratch_shapes=[pltpu.VMEM((1, LANES), F32)],
      compiler_params=_params("arbitrary", "arbitrary"),
      name="in_proj",
  )(x, sc, sh, w["g_mix"], head, head, head, head, *weights, *shared)


def _linear_scan(a, u, carry):
  ts, width = a.shape
  a = a.reshape(ts // SUBLANES, SUBLANES, width)
  u = u.reshape(ts // SUBLANES, SUBLANES, width)
  sub = lax.broadcasted_iota(jnp.int32, a.shape, 1)
  d = 1
  while d < SUBLANES:
    keep = sub >= d
    u = u + a * jnp.where(keep, pltpu.roll(u, d, 1), 0.0)
    a = a * jnp.where(keep, pltpu.roll(a, d, 1), 1.0)
    d *= 2
  groups = []
  for g in range(ts // SUBLANES):
    h = u[g] + a[g] * carry
    groups.append(h)
    carry = h[SUBLANES - 1:SUBLANES]
  return jnp.concatenate(groups, axis=0), carry


def _local_kernel(vg_ref, y_ref, xin_ref, ga_ref, gc_ref,
                  cw_ref, cb_ref, lng_ref, lnb_ref, pa_ref,
                  lw_ref, lb_ref, wri_ref, bri_ref, lam_ref, pc_ref,
                  o_ref, cbuf, csh, cvo, lbuf, hcar):
  ts = vg_ref.shape[0]
  t = pl.program_id(1)

  @pl.when(t == 0)
  def _():
    cbuf[0:CONV_HALO, :] = jnp.zeros((CONV_HALO, CONV_WIDTH), F32)
    lbuf[0:LRU_HALO, :] = jnp.zeros((LRU_HALO, LRU_WIDTH), F32)
    hcar[...] = jnp.zeros_like(hcar)

  @pl.when(t > 0)
  def _():
    cbuf[0:CONV_HALO, :] = cbuf[ts:ts + CONV_HALO, :]
    lbuf[0:LRU_HALO, :] = lbuf[ts:ts + LRU_HALO, :]

  cbuf[CONV_HALO:CONV_HALO + ts, :] = vg_ref[...]
  lbuf[LRU_HALO:LRU_HALO + ts, :] = xin_ref[...]

  first = CONV_HALO - (CONV_KERNEL - 1)
  for r in range(1, SUBLANES):
    csh[r - 1] = _shift_up(cbuf[...], r)
  for r0 in range(0, ts, CONV_ROWS):
    acc = jnp.broadcast_to(cb_ref[...], (CONV_ROWS, CONV_WIDTH))
    for j in range(CONV_KERNEL):
      r = (first + j) % SUBLANES
      lo = r0 + (first + j) - r
      win = cbuf[lo:lo + CONV_ROWS, :] if r == 0 else csh[r - 1, lo:lo + CONV_ROWS, :]
      acc = acc + cw_ref[j:j + 1, :] * win
    cvo[r0:r0 + CONV_ROWS, :] = acc

  cv = cvo[...]
  mu = jnp.mean(cv, axis=-1, keepdims=True)
  cen = cv - mu
  var = jnp.mean(cen * cen, axis=-1, keepdims=True)
  ln = (cen * lax.rsqrt(var + EPS)) * lng_ref[...] + lnb_ref[...]
  br_a = _dot((ln * _sigmoid(ln)).astype(BF16), pa_ref[...])

  first = LRU_HALO - (LRU_CONV - 1)
  xr = jnp.broadcast_to(lb_ref[...], (ts, LRU_WIDTH))
  xin = lbuf[...]
  for j in range(LRU_CONV):
    off = first + j
    win = xin[off:, :] if off % SUBLANES == 0 else _shift_up(xin, off)
    xr = xr + lw_ref[j:j + 1, :] * win[:ts]
  ri = _dot(xr.astype(BF16), wri_ref[...]) + bri_ref[...]
  r = _sigmoid(ri[:, :LRU_WIDTH])
  i = _sigmoid(ri[:, LRU_WIDTH:])
  log_a = (-LRU_C * r) * _softplus(-lam_ref[...])
  a = jnp.exp(log_a)
  th = jnp.tanh(log_a)
  u = jnp.sqrt(-2.0 * th / (1.0 - th)) * (i * xr)
  h, hcar[...] = _linear_scan(a, u, hcar[...])
  br_c = _dot((_gelu_tanh(y_ref[...]) * h).astype(BF16), pc_ref[...])

  o_ref[...] = ga_ref[...] * br_a + gc_ref[...] * br_c


def _local_call(l, vg, y, xin, gates, w):
  bsz, seq, _ = vg.shape
  d = D_MODEL
  ts = TS_LOCAL
  row = lambda width, col=0: pl.BlockSpec((None, ts, width), lambda b, t: (b, t, col))
  weights = [w["cw"], w["cb"], w["lng"], w["lnb"], w["pa"],
             w["lw"], w["lb"], w["wri"], w["bri"], w["lam"], w["pc"]]
  return pl.pallas_call(
      _local_kernel,
      out_shape=jax.ShapeDtypeStruct((bsz, seq, d), F32),
      grid=(bsz, seq // ts),
      in_specs=[row(CONV_WIDTH), row(LRU_WIDTH), row(LRU_WIDTH), row(d, 0), row(d, 2)]
      + [_layer_block(a, l) for a in weights],
      out_specs=row(d),
      scratch_shapes=[
          pltpu.VMEM((CONV_HALO + ts, CONV_WIDTH), F32),
          pltpu.VMEM((SUBLANES - 1, CONV_HALO + ts - SUBLANES, CONV_WIDTH), F32),
          pltpu.VMEM((ts, CONV_WIDTH), F32),
          pltpu.VMEM((LRU_HALO + ts, LRU_WIDTH), F32),
          pltpu.VMEM((1, LRU_WIDTH), F32),
      ],
      compiler_params=_params("arbitrary", "arbitrary"),
      name="local",
  )(vg, y, xin, gates, gates, *weights)


def _attn_kernel(q_ref, k_ref, qa_ref, ka_ref, v_ref, o_ref,
                 qh_ref, kk_ref, vaug_ref, m_ref, acc_ref):
  seq = q_ref.shape[0]
  lane = lax.broadcasted_iota(jnp.int32, (seq, LANES), 1)
  q2 = q_ref[...].astype(F32)
  qa2 = qa_ref[...].astype(F32)
  v2 = v_ref[...].astype(F32)
  for hh in range(2):
    own_dims = (lane < FOX_HEAD_DIM) if hh == 0 else (lane >= FOX_HEAD_DIM)
    own_bias = (lane < AUG_COLS) if hh == 0 else (lane >= AUG_COLS)
    qh_ref[hh, :, :LANES] = jnp.where(own_dims, q2, 0.0).astype(BF16)
    qh_ref[hh, :, LANES:] = jnp.where(own_bias, qa2, 0.0).astype(BF16)
    ones_lane = FOX_HEAD_DIM if hh == 0 else 0
    vaug_ref[hh] = jnp.where(own_dims, v2, jnp.where(lane == ones_lane, 1.0, 0.0)).astype(BF16)
  kk_ref[:, :LANES] = k_ref[...]
  kk_ref[:, LANES:] = ka_ref[...]
  tq = m_ref.shape[1]

  def update(hh, q0, k0, r0, diagonal):
    rows = tq - r0
    q = qh_ref[hh, pl.ds(q0 + r0, rows), :]
    k = kk_ref[pl.ds(k0, TK), :]
    s = lax.dot_general(q, k, (((1,), (1,)), ((), ())), preferred_element_type=F32)
    if diagonal:
      row = lax.broadcasted_iota(jnp.int32, (TK, TK), 0)
      col = lax.broadcasted_iota(jnp.int32, (TK, TK), 1)
      top = jnp.where(row >= col, s[:TK], NEG_BIG)
      s = top if rows == TK else jnp.concatenate([top, s[TK:]], axis=0)
    m_prev = m_ref[hh, r0:, :]
    m_next = jnp.maximum(m_prev, jnp.max(s, axis=1, keepdims=True))
    p = jnp.exp2(s - jnp.concatenate([m_next] * (TK // LANES), axis=1))
    alpha = jnp.exp2(m_prev - m_next)
    pv = _dot(p.astype(BF16), vaug_ref[hh, pl.ds(k0, TK), :])
    acc_ref[hh, r0:, :] = alpha * acc_ref[hh, r0:, :] + pv
    m_ref[hh, r0:, :] = m_next

  def q_tile(i, carry):
    q0 = pl.multiple_of(i * tq, tq)
    m_ref[...] = jnp.full(m_ref.shape, NEG_BIG, F32)
    acc_ref[...] = jnp.zeros_like(acc_ref)

    def kv_step(j, c):
      k0 = pl.multiple_of(j * TK, TK)
      for hh in range(2):
        update(hh, q0, k0, 0, False)
      return c

    lax.fori_loop(0, i * (tq // TK), kv_step, 0)
    for r0 in range(0, tq, TK):
      for hh in range(2):
        update(hh, q0, pl.multiple_of(q0 + r0, TK), r0, True)
    a0 = acc_ref[0]
    a1 = acc_ref[1]
    lane = lax.broadcasted_iota(jnp.int32, a0.shape, 1)
    o0 = a0 / a0[:, FOX_HEAD_DIM:FOX_HEAD_DIM + 1]
    o1 = a1 / a1[:, 0:1]
    o_ref[pl.ds(q0, tq), :] = jnp.where(lane < FOX_HEAD_DIM, o0, o1).astype(BF16)
    return carry

  lax.fori_loop(0, seq // tq, q_tile, 0)


def _attn_call(q, k, qa, ka, v):
  bsz, seq, _ = q.shape
  tq = min(TQ, seq)
  assert tq % TK == 0 and seq % tq == 0
  pair = pl.BlockSpec((None, seq, LANES), lambda b, h: (b, 0, h))
  return pl.pallas_call(
      _attn_kernel,
      out_shape=jax.ShapeDtypeStruct((bsz, seq, FOX_WIDTH), BF16),
      grid=(bsz, HEAD_PAIRS),
      in_specs=[pair] * 5,
      out_specs=pair,
      scratch_shapes=[
          pltpu.VMEM((2, seq, 2 * LANES), BF16),
          pltpu.VMEM((seq, 2 * LANES), BF16),
          pltpu.VMEM((2, seq, LANES), BF16),
          pltpu.VMEM((2, tq, LANES), F32),
          pltpu.VMEM((2, tq, LANES), F32),
      ],
      compiler_params=_params("arbitrary", "arbitrary"),
      name="attn",
  )(q, k, qa, ka, v)


def _mlp_kernel(x_ref, o_ref, gb_ref, mac_ref, gt1_ref, sc_ref, sh_ref, gt2_ref, g_ref,
                pb_ref, wo_ref, w1_ref, w2_ref, out_ref):
  br_b = _dot(o_ref[...], pb_ref[...])
  merged = mac_ref[...] + _sigmoid(gb_ref[...]) * br_b
  x1 = x_ref[...] + gt1_ref[...] * _dot(merged.astype(BF16), wo_ref[...])
  hb = _modulated_norm(x1, g_ref[...], sc_ref[...], sh_ref[...]).astype(BF16)
  ffn = None
  for c0, cn in FFN_CHUNKS:
    a = _dot(hb, w1_ref[:, c0:c0 + cn])
    b = _dot(hb, w1_ref[:, FFN_HIDDEN + c0:FFN_HIDDEN + c0 + cn])
    part = _dot(((a * _sigmoid(a)) * b).astype(BF16), w2_ref[c0:c0 + cn, :])
    ffn = part if ffn is None else ffn + part
  out_ref[...] = x1 + gt2_ref[...] * ffn


def _mlp_call(l, x, o, gates, mac, gt1, sc2, sh2, gt2, w):
  bsz, seq, d = x.shape
  ts = TS_MLP
  row = lambda width, col=0: pl.BlockSpec((None, ts, width), lambda b, t: (b, t, col))
  per_batch = pl.BlockSpec((None, 1, d), lambda b, t: (b, 0, 0))
  weights = [w["g_ffn"], w["pb"], w["wo"], w["w1"], w["w2"]]
  return pl.pallas_call(
      _mlp_kernel,
      out_shape=jax.ShapeDtypeStruct((bsz, seq, d), F32),
      grid=(bsz, seq // ts),
      in_specs=[row(d), row(FOX_WIDTH), row(d, 1), row(d),
                per_batch, per_batch, per_batch, per_batch]
      + [_layer_block(a, l) for a in weights],
      out_specs=row(d),
      compiler_params=_params("arbitrary", "arbitrary"),
      name="mlp",
  )(x, o, gates, mac, gt1, sc2, sh2, gt2, *weights)


def _block_diag(w):
  depth, n, a, b = w.shape
  eye = jnp.eye(n, dtype=w.dtype)
  return (eye[None, :, None, :, None] * w[:, :, :, None, :]).reshape(depth, n * a, n * b)


def _placement_consts(ts):
  plq = np.zeros((LANES, HEAD_PAIRS * LANES), np.float32)
  plk = np.zeros((LANES, HEAD_PAIRS * LANES), np.float32)
  one_lane = FOX_HEADS * CUM_PARTS
  for hh in range(FOX_HEADS):
    base = (hh // 2) * LANES + (hh % 2) * AUG_COLS
    for i in range(CUM_PARTS):
      plq[FOX_HEADS * i + hh, base + i] = 1.0
      plq[one_lane, base + CUM_PARTS + i] = 1.0
      plk[one_lane, base + i] = 1.0
      plk[FOX_HEADS * i + hh, base + CUM_PARTS + i] = -1.0
  tri = np.tril(np.ones((ts, ts), np.float32))
  return {"plq": jnp.asarray(plq, BF16), "plk": jnp.asarray(plk, BF16),
          "tri": jnp.asarray(tri, BF16)}


def _stacked_weights(p):
  w_in = p["w_in"]
  o_glu = 0
  o_q = o_glu + 2 * CONV_WIDTH
  o_k = o_q + FOX_WIDTH
  o_v = o_k + FOX_WIDTH
  o_f = o_v + FOX_WIDTH
  o_y = o_f + FOX_HEADS
  o_g = o_y + 2 * LRU_WIDTH
  assert (o_q, o_k, o_v) == (2 * FOX_WIDTH, 3 * FOX_WIDTH, 4 * FOX_WIDTH)
  pad_f = LANES - CUM_PARTS * FOX_HEADS
  wf = jnp.pad(jnp.tile(w_in[:, :, o_f:o_y], (1, 1, CUM_PARTS)), ((0, 0), (0, 0), (0, pad_f)))
  bf = jnp.pad(jnp.tile(p["fox_b_f"], (1, CUM_PARTS)), ((0, 0), (0, pad_f)))
  scale = FOX_HEAD_DIM ** -0.5 * LOG2E
  gq = jnp.tile(p["fox_q_norm_g"] * scale, (1, FOX_HEADS))
  gk = jnp.tile(p["fox_k_norm_g"], (1, FOX_HEADS))
  row = lambda v: v[:, None, :]
  return {
      "g_mix": row(p["g_norm_mix"]), "g_ffn": row(p["g_norm_ffn"]),
      "in_head": w_in[:, :, :o_f].astype(BF16),
      "f": wf.astype(BF16),
      "yx": w_in[:, :, o_y:o_g].astype(BF16),
      "gate": w_in[:, :, o_g:].astype(BF16),
      "bf": row(bf), "gq": row(gq), "gk": row(gk),
      "cw": p["conv_dw_w"], "cb": row(p["conv_dw_b"]),
      "lng": row(p["conv_ln_g"]), "lnb": row(p["conv_ln_b"]),
      "pa": p["w_conv_out"].astype(BF16),
      "lw": p["lru_conv_w"], "lb": row(p["lru_conv_b"]),
      "wri": jnp.concatenate([_block_diag(p["lru_w_r"]), _block_diag(p["lru_w_i"])],
                             axis=2).astype(BF16),
      "bri": row(jnp.concatenate([p["lru_b_r"], p["lru_b_i"]], axis=1)),
      "lam": row(p["lru_lambda"]),
      "pc": p["w_lru_out"].astype(BF16),
      "pb": p["w_fox_out"].astype(BF16),
      "wo": p["w_o"].astype(BF16),
      "w1": p["w_ffn_in"].astype(BF16),
      "w2": p["w_ffn_out"].astype(BF16),
  }


def kernel(x, c, w_ada, b_ada, g_norm_mix, g_norm_ffn, w_in, conv_dw_w, conv_dw_b, conv_ln_g, conv_ln_b, w_conv_out, fox_b_f, fox_q_norm_g, fox_k_norm_g, w_fox_out, lru_conv_w, lru_conv_b, lru_w_r, lru_b_r, lru_w_i, lru_b_i, lru_lambda, w_lru_out, w_o, w_ffn_in, w_ffn_out):
  p = dict(g_norm_mix=g_norm_mix, g_norm_ffn=g_norm_ffn, w_in=w_in,
           conv_dw_w=conv_dw_w, conv_dw_b=conv_dw_b, conv_ln_g=conv_ln_g,
           conv_ln_b=conv_ln_b, w_conv_out=w_conv_out, fox_b_f=fox_b_f,
           fox_q_norm_g=fox_q_norm_g, fox_k_norm_g=fox_k_norm_g, w_fox_out=w_fox_out,
           lru_conv_w=lru_conv_w, lru_conv_b=lru_conv_b, lru_w_r=lru_w_r, lru_b_r=lru_b_r,
           lru_w_i=lru_w_i, lru_b_i=lru_b_i, lru_lambda=lru_lambda, w_lru_out=w_lru_out,
           w_o=w_o, w_ffn_in=w_ffn_in, w_ffn_out=w_ffn_out)
  bsz, seq, d = x.shape
  depth = w_ada.shape[0]
  assert d == D_MODEL and seq % max(TS_PROJ, TS_LOCAL, TS_MLP) == 0
  consts = _placement_consts(TS_PROJ)
  w = _stacked_weights(p)
  mod = _ada_call(c, w_ada, b_ada)
  for l in range(depth):
    sh1, sc1, gt1, sh2, sc2, gt2 = [m[:, None, :] for m in jnp.split(mod[l], 6, axis=-1)]
    vg, q, k, qa, ka, v, y, xin, gates = _inproj_call(l, x, sc1, sh1, w, consts)
    mac = _local_call(l, vg, y, xin, gates, w)
    o = _attn_call(q, k, qa, ka, v)
    x = _mlp_call(l, x, o, gates, mac, gt1, sc2, sh2, gt2, w)
  return x
```

```python
import jax
import jax.numpy as jnp
import numpy as np
from jax import lax
from jax.experimental import pallas as pl
from jax.experimental.pallas import tpu as pltpu

F32 = jnp.float32
BF16 = jnp.bfloat16

D_MODEL = 1024
EPS = 1e-6
CONV_WIDTH = 512
CONV_KERNEL = 31
FOX_HEADS = 8
FOX_HEAD_DIM = 64
FOX_WIDTH = FOX_HEADS * FOX_HEAD_DIM
LRU_WIDTH = 512
LRU_BLOCKS = 8
LRU_CONV = 4
LRU_C = 8.0
N_BRANCH = 3
FFN_HIDDEN = 2816

LANES = 128
SUBLANES = 8
HEAD_PAIRS = FOX_HEADS // 2
CUM_PARTS = 3
AUG_COLS = 2 * CUM_PARTS
CONV_HALO = 32
LRU_HALO = SUBLANES
VMEM_LIMIT = 56 * 1024 * 1024
LOG2E = float(np.log2(np.e))
NEG_BIG = -1e30

TS_PROJ = 256
TS_LOCAL = 512
TQ = 2048
TK = 512
TS_MLP = 512
CONV_ROWS = 64
FFN_CHUNKS = ((0, 1024), (1024, 1024), (2048, 768))


def _dot(a, b):
  return jnp.dot(a, b, preferred_element_type=F32)


def _sigmoid(x):
  return 1.0 / (1.0 + jnp.exp(-x))


def _softplus(x):
  return jnp.maximum(x, 0.0) + jnp.log1p(jnp.exp(-jnp.abs(x)))


def _split_bf16(x, parts):
  out = []
  rem = x
  for _ in range(parts):
    p = rem.astype(BF16)
    out.append(p)
    rem = rem - p.astype(F32)
  return out


def _resident(shape):
  nd = len(shape)
  return pl.BlockSpec(shape, lambda *_: (0,) * nd, pipeline_mode=pl.Buffered(1))


def _layer_block(stacked, l):
  shape = stacked.shape[1:]
  nd = len(shape)
  return pl.BlockSpec((None,) + shape, lambda *_: (l,) + (0,) * nd,
                      pipeline_mode=pl.Buffered(1))


def _layer_cols(stacked, l, width, idx):
  rows = stacked.shape[1]
  return pl.BlockSpec((None, rows, width), lambda *_: (l, 0, idx),
                      pipeline_mode=pl.Buffered(1))


def _params(*sem):
  return pltpu.CompilerParams(dimension_semantics=sem, vmem_limit_bytes=VMEM_LIMIT)


def _ada_kernel(c_ref, w_ref, b_ref, o_ref):
  c = c_ref[...]
  act = (c * _sigmoid(c)).astype(BF16)
  o_ref[...] = _dot(act, w_ref[...].astype(BF16)) + b_ref[...]


def _ada_call(c, w_ada, b_ada):
  depth, d, n = w_ada.shape
  bsz = c.shape[0]
  tn = d
  return pl.pallas_call(
      _ada_kernel,
      out_shape=jax.ShapeDtypeStruct((depth, bsz, n), F32),
      grid=(depth, n // tn),
      in_specs=[
          pl.BlockSpec((bsz, d), lambda l, j: (0, 0)),
          pl.BlockSpec((None, d, tn), lambda l, j: (l, 0, j)),
          pl.BlockSpec((None, 1, tn), lambda l, j: (l, 0, j)),
      ],
      out_specs=pl.BlockSpec((None, bsz, tn), lambda l, j: (l, 0, j)),
      compiler_params=_params("arbitrary", "arbitrary"),
      name="ada",
  )(c, w_ada, b_ada.reshape(depth, 1, n))


def _modulated_norm(x, g, sc, sh):
  ms = jnp.mean(x * x, axis=-1, keepdims=True)
  h = (x * lax.rsqrt(ms + EPS)) * g
  return h * (1.0 + sc) + sh


def _head_norm(z):
  parts = []
  for hp in range(HEAD_PAIRS):
    s = z[:, hp * LANES:(hp + 1) * LANES]
    low = lax.broadcasted_iota(jnp.int32, s.shape, 1) < FOX_HEAD_DIM
    sq = s * s
    ssq = jnp.where(low,
                    jnp.sum(jnp.where(low, sq, 0.0), axis=-1, keepdims=True),
                    jnp.sum(jnp.where(low, 0.0, sq), axis=-1, keepdims=True))
    parts.append(s * lax.rsqrt(ssq * (1.0 / FOX_HEAD_DIM) + EPS))
  return jnp.concatenate(parts, axis=-1)


def _inproj_kernel(x_ref, sc_ref, sh_ref, g_ref,
                   wglu_ref, wq_ref, wk_ref, wv_ref, wf_ref, wyx_ref, wg_ref,
                   bf_ref, gq_ref, gk_ref, tri_ref, plq_ref, plk_ref,
                   vg_ref, q_ref, k_ref, qa_ref, ka_ref, v_ref, y_ref, xin_ref, gate_ref,
                   carry_ref):
  ts = x_ref.shape[0]
  t = pl.program_id(1)
  hb = _modulated_norm(x_ref[...], g_ref[...], sc_ref[...], sh_ref[...]).astype(BF16)

  zg = _dot(hb, wglu_ref[...])
  vg_ref[...] = zg[:, :CONV_WIDTH] * _sigmoid(zg[:, CONV_WIDTH:])

  zf = _dot(hb, wf_ref[...]) + bf_ref[...]
  lf = jnp.minimum(zf, 0.0) - jnp.log1p(jnp.exp(-jnp.abs(zf)))

  @pl.when(t == 0)
  def _():
    carry_ref[...] = jnp.zeros_like(carry_ref)

  tri = tri_ref[...]
  cum = carry_ref[...]
  for piece in _split_bf16(lf, CUM_PARTS):
    cum = cum + _dot(tri, piece)
  carry_ref[...] = cum[ts - 1:ts, :]

  lane = lax.broadcasted_iota(jnp.int32, cum.shape, 1)
  pieces = _split_bf16(cum * LOG2E, CUM_PARTS)
  src = jnp.where(lane == FOX_HEADS * CUM_PARTS, 1.0, 0.0)
  for i in reversed(range(CUM_PARTS)):
    src = jnp.where(lane < FOX_HEADS * (i + 1), pieces[i].astype(F32), src)
  src = src.astype(BF16)

  q_ref[...] = (_head_norm(_dot(hb, wq_ref[...])) * gq_ref[...]).astype(BF16)
  k_ref[...] = (_head_norm(_dot(hb, wk_ref[...])) * gk_ref[...]).astype(BF16)
  qa_ref[...] = _dot(src, plq_ref[...]).astype(BF16)
  ka_ref[...] = _dot(src, plk_ref[...]).astype(BF16)

  v_ref[...] = _dot(hb, wv_ref[...]).astype(BF16)

  zyx = _dot(hb, wyx_ref[...])
  y_ref[...] = zyx[:, :LRU_WIDTH]
  xin_ref[...] = zyx[:, LRU_WIDTH:]

  gate_ref[...] = _sigmoid(_dot(hb, wg_ref[...]))


def _inproj_call(l, x, sc, sh, w, consts):
  bsz, seq, d = x.shape
  ts = TS_PROJ
  grid = (bsz, seq // ts)
  row = lambda width: pl.BlockSpec((None, ts, width), lambda b, t: (b, t, 0))
  per_batch = pl.BlockSpec((None, 1, d), lambda b, t: (b, 0, 0))
  head = w["in_head"]
  head_cols = [(2 * CONV_WIDTH, 0), (FOX_WIDTH, 2), (FOX_WIDTH, 3), (FOX_WIDTH, 4)]
  weights = [w["f"], w["yx"], w["gate"], w["bf"], w["gq"], w["gk"]]
  shared = [consts["tri"], consts["plq"], consts["plk"]]
  out_widths = [(CONV_WIDTH, F32), (FOX_WIDTH, BF16), (FOX_WIDTH, BF16),
                (HEAD_PAIRS * LANES, BF16), (HEAD_PAIRS * LANES, BF16), (FOX_WIDTH, BF16),
                (LRU_WIDTH, F32), (LRU_WIDTH, F32), (N_BRANCH * d, F32)]
  return pl.pallas_call(
      _inproj_kernel,
      out_shape=[jax.ShapeDtypeStruct((bsz, seq, wd), dt) for wd, dt in out_widths],
      grid=grid,
      in_specs=[row(d), per_batch, per_batch, _layer_block(w["g_mix"], l)]
      + [_layer_cols(head, l, width, idx) for width, idx in head_cols]
      + [_layer_block(a, l) for a in weights] + [_resident(a.shape) for a in shared],
      out_specs=[row(wd) for wd, _ in out_widths],
      scratch_shapes=[pltpu.VMEM((1, LANES), F32)],
      compiler_params=_params("arbitrary", "arbitrary"),
      name="in_proj",
  )(x, sc, sh, w["g_mix"], head, head, head, head, *weights, *shared)


def _gelu_tanh(x):
  c = np.float32(np.sqrt(2.0 / np.pi))
  return 0.5 * x * (1.0 + jnp.tanh(c * (x + 0.044715 * (x * x * x))))


def _shift_up(x, r):
  rows, width = x.shape
  x3 = x.reshape(rows // SUBLANES, SUBLANES, width)
  sub = lax.broadcasted_iota(jnp.int32, (rows // SUBLANES - 1, SUBLANES, width), 1)
  mix = jnp.where(sub >= r, x3[:-1], x3[1:])
  out = pltpu.roll(mix, SUBLANES - r, 1)
  return out.reshape(rows - SUBLANES, width)


def _linear_scan(a, u, carry):
  ts, width = a.shape
  a = a.reshape(ts // SUBLANES, SUBLANES, width)
  u = u.reshape(ts // SUBLANES, SUBLANES, width)
  sub = lax.broadcasted_iota(jnp.int32, a.shape, 1)
  d = 1
  while d < SUBLANES:
    keep = sub >= d
    u = u + a * jnp.where(keep, pltpu.roll(u, d, 1), 0.0)
    a = a * jnp.where(keep, pltpu.roll(a, d, 1), 1.0)
    d *= 2
  groups = []
  for g in range(ts // SUBLANES):
    h = u[g] + a[g] * carry
    groups.append(h)
    carry = h[SUBLANES - 1:SUBLANES]
  return jnp.concatenate(groups, axis=0), carry


def _local_kernel(vg_ref, y_ref, xin_ref, ga_ref, gc_ref,
                  cw_ref, cb_ref, lng_ref, lnb_ref, pa_ref,
                  lw_ref, lb_ref, wri_ref, bri_ref, lam_ref, pc_ref,
                  o_ref, cbuf, csh, cvo, cwb, lbuf, hcar):
  ts = vg_ref.shape[0]
  t = pl.program_id(1)

  @pl.when(t == 0)
  def _():
    cbuf[0:CONV_HALO, :] = jnp.zeros((CONV_HALO, CONV_WIDTH), F32)
    lbuf[0:LRU_HALO, :] = jnp.zeros((LRU_HALO, LRU_WIDTH), F32)
    hcar[...] = jnp.zeros_like(hcar)

  @pl.when(t > 0)
  def _():
    cbuf[0:CONV_HALO, :] = cbuf[ts:ts + CONV_HALO, :]
    lbuf[0:LRU_HALO, :] = lbuf[ts:ts + LRU_HALO, :]

  cbuf[CONV_HALO:CONV_HALO + ts, :] = vg_ref[...]
  lbuf[LRU_HALO:LRU_HALO + ts, :] = xin_ref[...]

  first = CONV_HALO - (CONV_KERNEL - 1)
  for r in range(1, SUBLANES):
    csh[r - 1] = _shift_up(cbuf[...], r)
  for j in range(CONV_KERNEL):
    cwb[j] = jnp.broadcast_to(cw_ref[j:j + 1, :], (SUBLANES, CONV_WIDTH))
  for r0 in range(0, ts, CONV_ROWS):
    acc = jnp.broadcast_to(cb_ref[...], (CONV_ROWS, CONV_WIDTH))
    for j in range(CONV_KERNEL):
      r = (first + j) % SUBLANES
      lo = r0 + (first + j) - r
      win = cbuf[lo:lo + CONV_ROWS, :] if r == 0 else csh[r - 1, lo:lo + CONV_ROWS, :]
      acc = acc + jnp.concatenate([cwb[j]] * (CONV_ROWS // SUBLANES), axis=0) * win
    cvo[r0:r0 + CONV_ROWS, :] = acc

  cv = cvo[...]
  mu = jnp.mean(cv, axis=-1, keepdims=True)
  cen = cv - mu
  var = jnp.mean(cen * cen, axis=-1, keepdims=True)
  ln = (cen * lax.rsqrt(var + EPS)) * lng_ref[...] + lnb_ref[...]
  br_a = _dot((ln * _sigmoid(ln)).astype(BF16), pa_ref[...])

  first = LRU_HALO - (LRU_CONV - 1)
  xr = jnp.broadcast_to(lb_ref[...], (ts, LRU_WIDTH))
  xin = lbuf[...]
  for j in range(LRU_CONV):
    off = first + j
    win = xin[off:, :] if off % SUBLANES == 0 else _shift_up(xin, off)
    xr = xr + lw_ref[j:j + 1, :] * win[:ts]
  ri = _dot(xr.astype(BF16), wri_ref[...]) + bri_ref[...]
  r = _sigmoid(ri[:, :LRU_WIDTH])
  i = _sigmoid(ri[:, LRU_WIDTH:])
  log_a = (-LRU_C * r) * _softplus(-lam_ref[...])
  a = jnp.exp(log_a)
  th = jnp.tanh(log_a)
  u = jnp.sqrt(-2.0 * th / (1.0 - th)) * (i * xr)
  h, hcar[...] = _linear_scan(a, u, hcar[...])
  br_c = _dot((_gelu_tanh(y_ref[...]) * h).astype(BF16), pc_ref[...])

  o_ref[...] = ga_ref[...] * br_a + gc_ref[...] * br_c


def _local_call(l, vg, y, xin, gates, w):
  bsz, seq, _ = vg.shape
  d = D_MODEL
  ts = TS_LOCAL
  row = lambda width, col=0: pl.BlockSpec((None, ts, width), lambda b, t: (b, t, col))
  weights = [w["cw"], w["cb"], w["lng"], w["lnb"], w["pa"],
             w["lw"], w["lb"], w["wri"], w["bri"], w["lam"], w["pc"]]
  return pl.pallas_call(
      _local_kernel,
      out_shape=jax.ShapeDtypeStruct((bsz, seq, d), F32),
      grid=(bsz, seq // ts),
      in_specs=[row(CONV_WIDTH), row(LRU_WIDTH), row(LRU_WIDTH), row(d, 0), row(d, 2)]
      + [_layer_block(a, l) for a in weights],
      out_specs=row(d),
      scratch_shapes=[
          pltpu.VMEM((CONV_HALO + ts, CONV_WIDTH), F32),
          pltpu.VMEM((SUBLANES - 1, CONV_HALO + ts - SUBLANES, CONV_WIDTH), F32),
          pltpu.VMEM((ts, CONV_WIDTH), F32),
          pltpu.VMEM((CONV_KERNEL, SUBLANES, CONV_WIDTH), F32),
          pltpu.VMEM((LRU_HALO + ts, LRU_WIDTH), F32),
          pltpu.VMEM((1, LRU_WIDTH), F32),
      ],
      compiler_params=_params("arbitrary", "arbitrary"),
      name="local",
  )(vg, y, xin, gates, gates, *weights)


def _attn_kernel(q_ref, k_ref, qa_ref, ka_ref, v_ref, o_ref,
                 qh_ref, kk_ref, vaug_ref, m_ref, acc_ref):
  seq = q_ref.shape[0]
  lane = lax.broadcasted_iota(jnp.int32, (seq, LANES), 1)
  q2 = q_ref[...].astype(F32)
  qa2 = qa_ref[...].astype(F32)
  v2 = v_ref[...].astype(F32)
  for hh in range(2):
    own_dims = (lane < FOX_HEAD_DIM) if hh == 0 else (lane >= FOX_HEAD_DIM)
    own_bias = (lane < AUG_COLS) if hh == 0 else (lane >= AUG_COLS)
    qh_ref[hh, :, :LANES] = jnp.where(own_dims, q2, 0.0).astype(BF16)
    qh_ref[hh, :, LANES:] = jnp.where(own_bias, qa2, 0.0).astype(BF16)
    ones_lane = FOX_HEAD_DIM if hh == 0 else 0
    vaug_ref[hh] = jnp.where(own_dims, v2, jnp.where(lane == ones_lane, 1.0, 0.0)).astype(BF16)
  kk_ref[:, :LANES] = k_ref[...]
  kk_ref[:, LANES:] = ka_ref[...]
  tq = m_ref.shape[1]

  def update(hh, q0, k0, r0, diagonal):
    rows = tq - r0
    q = qh_ref[hh, pl.ds(q0 + r0, rows), :]
    k = kk_ref[pl.ds(k0, TK), :]
    s = lax.dot_general(q, k, (((1,), (1,)), ((), ())), preferred_element_type=F32)
    if diagonal:
      row = lax.broadcasted_iota(jnp.int32, (TK, TK), 0)
      col = lax.broadcasted_iota(jnp.int32, (TK, TK), 1)
      top = jnp.where(row >= col, s[:TK], NEG_BIG)
      s = top if rows == TK else jnp.concatenate([top, s[TK:]], axis=0)
    m_prev = m_ref[hh, r0:, :]
    m_next = jnp.maximum(m_prev, jnp.max(s, axis=1, keepdims=True))
    p = jnp.exp2(s - jnp.concatenate([m_next] * (TK // LANES), axis=1))
    alpha = jnp.exp2(m_prev - m_next)
    pv = _dot(p.astype(BF16), vaug_ref[hh, pl.ds(k0, TK), :])
    acc_ref[hh, r0:, :] = alpha * acc_ref[hh, r0:, :] + pv
    m_ref[hh, r0:, :] = m_next

  def q_tile(i, carry):
    q0 = pl.multiple_of(i * tq, tq)
    m_ref[...] = jnp.full(m_ref.shape, NEG_BIG, F32)
    acc_ref[...] = jnp.zeros_like(acc_ref)

    def kv_step(j, c):
      k0 = pl.multiple_of(j * TK, TK)
      for hh in range(2):
        update(hh, q0, k0, 0, False)
      return c

    lax.fori_loop(0, i * (tq // TK), kv_step, 0)
    for r0 in range(0, tq, TK):
      for hh in range(2):
        update(hh, q0, pl.multiple_of(q0 + r0, TK), r0, True)
    a0 = acc_ref[0]
    a1 = acc_ref[1]
    lane = lax.broadcasted_iota(jnp.int32, a0.shape, 1)
    o0 = a0 / a0[:, FOX_HEAD_DIM:FOX_HEAD_DIM + 1]
    o1 = a1 / a1[:, 0:1]
    o_ref[pl.ds(q0, tq), :] = jnp.where(lane < FOX_HEAD_DIM, o0, o1).astype(BF16)
    return carry

  lax.fori_loop(0, seq // tq, q_tile, 0)


def _attn_call(q, k, qa, ka, v):
  bsz, seq, _ = q.shape
  tq = min(TQ, seq)
  assert tq % TK == 0 and seq % tq == 0
  pair = pl.BlockSpec((None, seq, LANES), lambda b, h: (b, 0, h))
  return pl.pallas_call(
      _attn_kernel,
      out_shape=jax.ShapeDtypeStruct((bsz, seq, FOX_WIDTH), BF16),
      grid=(bsz, HEAD_PAIRS),
      in_specs=[pair] * 5,
      out_specs=pair,
      scratch_shapes=[
          pltpu.VMEM((2, seq, 2 * LANES), BF16),
          pltpu.VMEM((seq, 2 * LANES), BF16),
          pltpu.VMEM((2, seq, LANES), BF16),
          pltpu.VMEM((2, tq, LANES), F32),
          pltpu.VMEM((2, tq, LANES), F32),
      ],
      compiler_params=_params("arbitrary", "arbitrary"),
      name="attn",
  )(q, k, qa, ka, v)


def _mlp_kernel(x_ref, o_ref, gb_ref, mac_ref, gt1_ref, sc_ref, sh_ref, gt2_ref, g_ref,
                pb_ref, wo_ref, w1_ref, w2_ref, out_ref):
  br_b = _dot(o_ref[...], pb_ref[...])
  merged = mac_ref[...] + gb_ref[...] * br_b
  x1 = x_ref[...] + gt1_ref[...] * _dot(merged.astype(BF16), wo_ref[...])
  hb = _modulated_norm(x1, g_ref[...], sc_ref[...], sh_ref[...]).astype(BF16)
  ffn = None
  for c0, cn in FFN_CHUNKS:
    a = _dot(hb, w1_ref[:, c0:c0 + cn])
    b = _dot(hb, w1_ref[:, FFN_HIDDEN + c0:FFN_HIDDEN + c0 + cn])
    part = _dot(((a * _sigmoid(a)) * b).astype(BF16), w2_ref[c0:c0 + cn, :])
    ffn = part if ffn is None else ffn + part
  out_ref[...] = x1 + gt2_ref[...] * ffn


def _mlp_call(l, x, o, gates, mac, gt1, sc2, sh2, gt2, w):
  bsz, seq, d = x.shape
  ts = TS_MLP
  row = lambda width, col=0: pl.BlockSpec((None, ts, width), lambda b, t: (b, t, col))
  per_batch = pl.BlockSpec((None, 1, d), lambda b, t: (b, 0, 0))
  weights = [w["g_ffn"], w["pb"], w["wo"], w["w1"], w["w2"]]
  return pl.pallas_call(
      _mlp_kernel,
      out_shape=jax.ShapeDtypeStruct((bsz, seq, d), F32),
      grid=(bsz, seq // ts),
      in_specs=[row(d), row(FOX_WIDTH), row(d, 1), row(d),
                per_batch, per_batch, per_batch, per_batch]
      + [_layer_block(a, l) for a in weights],
      out_specs=row(d),
      compiler_params=_params("arbitrary", "arbitrary"),
      name="mlp",
  )(x, o, gates, mac, gt1, sc2, sh2, gt2, *weights)


def _block_diag(w):
  depth, n, a, b = w.shape
  eye = jnp.eye(n, dtype=w.dtype)
  return (eye[None, :, None, :, None] * w[:, :, :, None, :]).reshape(depth, n * a, n * b)


def _placement_consts(ts):
  plq = np.zeros((LANES, HEAD_PAIRS * LANES), np.float32)
  plk = np.zeros((LANES, HEAD_PAIRS * LANES), np.float32)
  one_lane = FOX_HEADS * CUM_PARTS
  for hh in range(FOX_HEADS):
    base = (hh // 2) * LANES + (hh % 2) * AUG_COLS
    for i in range(CUM_PARTS):
      plq[FOX_HEADS * i + hh, base + i] = 1.0
      plq[one_lane, base + CUM_PARTS + i] = 1.0
      plk[one_lane, base + i] = 1.0
      plk[FOX_HEADS * i + hh, base + CUM_PARTS + i] = -1.0
  tri = np.tril(np.ones((ts, ts), np.float32))
  return {"plq": jnp.asarray(plq, BF16), "plk": jnp.asarray(plk, BF16),
          "tri": jnp.asarray(tri, BF16)}


def _stacked_weights(p):
  w_in = p["w_in"]
  o_glu = 0
  o_q = o_glu + 2 * CONV_WIDTH
  o_k = o_q + FOX_WIDTH
  o_v = o_k + FOX_WIDTH
  o_f = o_v + FOX_WIDTH
  o_y = o_f + FOX_HEADS
  o_g = o_y + 2 * LRU_WIDTH
  assert (o_q, o_k, o_v) == (2 * FOX_WIDTH, 3 * FOX_WIDTH, 4 * FOX_WIDTH)
  pad_f =LANES - CUM_PARTS * FOX_HEADS
  wf = jnp.pad(jnp.tile(w_in[:, :, o_f:o_y], (1, 1, CUM_PARTS)), ((0, 0), (0, 0), (0, pad_f)))
  bf = jnp.pad(jnp.tile(p["fox_b_f"], (1, CUM_PARTS)), ((0, 0), (0, pad_f)))
  scale = FOX_HEAD_DIM ** -0.5 * LOG2E
  gq = jnp.tile(p["fox_q_norm_g"] * scale, (1, FOX_HEADS))
  gk = jnp.tile(p["fox_k_norm_g"], (1, FOX_HEADS))
  row = lambda v: v[:, None, :]
  w_in_b = w_in.astype(BF16)
  return {
      "g_mix": row(p["g_norm_mix"]), "g_ffn": row(p["g_norm_ffn"]),
      "in_head": w_in_b,
      "f": wf.astype(BF16),
      "yx": w_in_b[:, :, o_y:o_g],
      "gate": w_in_b[:, :, o_g:],
      "bf": row(bf), "gq": row(gq), "gk": row(gk),
      "cw": p["conv_dw_w"], "cb": row(p["conv_dw_b"]),
      "lng": row(p["conv_ln_g"]), "lnb": row(p["conv_ln_b"]),
      "pa": p["w_conv_out"].astype(BF16),
      "lw": p["lru_conv_w"], "lb": row(p["lru_conv_b"]),
      "wri": jnp.concatenate([_block_diag(p["lru_w_r"]), _block_diag(p["lru_w_i"])],
                             axis=2).astype(BF16),
      "bri": row(jnp.concatenate([p["lru_b_r"], p["lru_b_i"]], axis=1)),
      "lam": row(p["lru_lambda"]),
      "pc": p["w_lru_out"].astype(BF16),
      "pb": p["w_fox_out"].astype(BF16),
      "wo": p["w_o"].astype(BF16),
      "w1": p["w_ffn_in"].astype(BF16),
      "w2": p["w_ffn_out"].astype(BF16),
  }


def kernel(x, c, w_ada, b_ada, g_norm_mix, g_norm_ffn, w_in, conv_dw_w, conv_dw_b, conv_ln_g, conv_ln_b, w_conv_out, fox_b_f, fox_q_norm_g, fox_k_norm_g, w_fox_out, lru_conv_w, lru_conv_b, lru_w_r, lru_b_r, lru_w_i, lru_b_i, lru_lambda, w_lru_out, w_o, w_ffn_in, w_ffn_out):
  p = dict(g_norm_mix=g_norm_mix, g_norm_ffn=g_norm_ffn, w_in=w_in,
           conv_dw_w=conv_dw_w, conv_dw_b=conv_dw_b, conv_ln_g=conv_ln_g,
           conv_ln_b=conv_ln_b, w_conv_out=w_conv_out, fox_b_f=fox_b_f,
           fox_q_norm_g=fox_q_norm_g, fox_k_norm_g=fox_k_norm_g, w_fox_out=w_fox_out,
           lru_conv_w=lru_conv_w, lru_conv_b=lru_conv_b, lru_w_r=lru_w_r, lru_b_r=lru_b_r,
           lru_w_i=lru_w_i, lru_b_i=lru_b_i, lru_lambda=lru_lambda, w_lru_out=w_lru_out,
           w_o=w_o, w_ffn_in=w_ffn_in, w_ffn_out=w_ffn_out)
  bsz, seq, d = x.shape
  depth = w_ada.shape[0]
  assert d == D_MODEL and seq % max(TS_PROJ, TS_LOCAL, TS_MLP) == 0
  consts = _placement_consts(TS_PROJ)
  w = _stacked_weights(p)
  mod = _ada_call(c, w_ada, b_ada)
  for l in range(depth):
    sh1, sc1, gt1, sh2, sc2, gt2 = [m[:, None, :] for m in jnp.split(mod[l], 6, axis=-1)]
    vg, q, k, qa, ka, v, y, xin, gates = _inproj_call(l, x, sc1, sh1, w, consts)
    mac = _local_call(l, vg, y, xin, gates, w)
    o = _attn_call(q, k, qa, ka, v)
    x = _mlp_call(l, x, o, gates, mac, gt1, sc2, sh2, gt2, w)
  return x
```
